```python
import math
import jax
import jax.numpy as jnp
from jax import lax
import numpy as np

D_MODEL = 2048
BATCH = 8
SEQ = 2048
DEPTH = 1
DEC_BATCH = 32
DEC_SEQ = 4
PAST_LEN = 16384
PAGE_SIZE = 128

H_SB = 8
DH_SB = 128
D_SB = H_SB * DH_SB
H_DF = 4
DH_DF = 128
D_DF = H_DF * 2 * DH_DF
PROJ_DIM = 3 * D_SB + 3 * D_DF + 2 * D_MODEL
Q_BLOCK = 128
PEER_HEADS = 8
N_KEYS = 128
N_EXPERTS = N_KEYS * N_KEYS
D_KEY = 256
PEER_TOPK = 16
TOK_BLOCK = 128
LAMBDA_STD = 0.1
EPS = 1e-6
POOL_NUM = 5
POOL_DEN = 4

kernel_name = 'stickbreak_diffattn_peer_hybrid_step'

F32 = jnp.float32


def _rmsnorm(x, g):
    xf = x.astype(F32)
    y = xf * lax.rsqrt(jnp.mean(xf * xf, axis=-1, keepdims=True) + EPS)
    return (y * g.astype(F32)).astype(x.dtype)


def _lambda_init(layer):
    return 0.8 - 0.6 * math.exp(-0.3 * layer)


def _alibi_slopes():
    return 2.0 ** (-8.0 * jnp.arange(1, H_DF + 1, dtype=F32) / H_DF)


def _mixer_inputs(x, norm_g, w_in, q_g, k_g):
    B, T = x.shape[:2]
    h = _rmsnorm(x, norm_g)
    p = jnp.einsum('btd,de->bte', h, w_in)
    cuts = [D_SB, 2 * D_SB, 3 * D_SB, 3 * D_SB + D_DF, 3 * D_SB + 2 * D_DF, 3 * D_SB + 3 * D_DF,
            3 * D_SB + 3 * D_DF + D_MODEL]
    sq, sk, sv, dq, dk, dv, ga, gb = jnp.split(p, cuts, axis=-1)
    sq = sq.reshape(B, T, H_SB, DH_SB)
    sk = sk.reshape(B, T, H_SB, DH_SB)
    sv = sv.reshape(B, T, H_SB, DH_SB)
    dq = _rmsnorm(dq.reshape(B, T, H_DF, 2, DH_DF), q_g)
    dk = _rmsnorm(dk.reshape(B, T, H_DF, 2, DH_DF), k_g)
    dv = dv.reshape(B, T, H_DF, 2 * DH_DF)
    return sq, sk, sv, dq, dk, dv, ga, gb


def _stick_breaking(q, k, v, q_pos, k_pos):
    z = jnp.einsum('bqhd,bkhd->bhqk', q.astype(F32), k.astype(F32)) * (DH_SB ** -0.5)
    mask = k_pos[None, :] < q_pos[:, None]
    log_rem = jnp.where(mask, jax.nn.log_sigmoid(-z), 0.0)
    after = lax.cumsum(log_rem, axis=3, reverse=True) - log_rem
    w = jnp.where(mask, jnp.exp(jax.nn.log_sigmoid(z) + after), 0.0)
    return jnp.einsum('bhqk,bkhd->bqhd', w, v.astype(F32)).astype(v.dtype)


def _diff_attention(q, k, v, q_pos, k_pos, lam, subln_g, lambda_init):
    s = jnp.einsum('bqhcd,bkhcd->bhcqk', q.astype(F32), k.astype(F32)) * (DH_DF ** -0.5)
    dist = (q_pos[:, None] - k_pos[None, :]).astype(F32)
    s = s - _alibi_slopes()[None, :, None, None, None] * dist
    s = jnp.where(k_pos[None, :] <= q_pos[:, None], s, -jnp.inf)
    p = jax.nn.softmax(s, axis=-1)
    a = p[:, :, 0] - lam * p[:, :, 1]
    o = jnp.einsum('bhqk,bkhe->bqhe', a, v.astype(F32))
    o = _rmsnorm(o, subln_g) * (1.0 - lambda_init)
    return o.astype(v.dtype)


def _prompt_attention(sq, sk, sv, dq, dk, dv, lam, subln_g, lambda_init):
    B, T = sq.shape[:2]
    nb = T // Q_BLOCK
    k_pos = jnp.arange(T, dtype=jnp.int32)

    def to_blocks(a):
        return jnp.moveaxis(a.reshape((B, nb, Q_BLOCK) + a.shape[2:]), 1, 0)

    def from_blocks(a):
        return jnp.moveaxis(a, 0, 1).reshape((B, T) + a.shape[3:])

    def block(args):
        i, sqb, dqb = args
        q_pos = i * Q_BLOCK + jnp.arange(Q_BLOCK, dtype=jnp.int32)
        return (_stick_breaking(sqb, sk, sv, q_pos, k_pos),
                _diff_attention(dqb, dk, dv, q_pos, k_pos, lam, subln_g, lambda_init))

    osb, odf = lax.map(block, (jnp.arange(nb, dtype=jnp.int32), to_blocks(sq), to_blocks(dq)))
    return from_blocks(osb), from_blocks(odf)


def _sample_attention(sq, sk, sv, dq, dk, dv, pool_sb_k, pool_sb_v, pool_df_k, pool_df_v,
                      page_table, lam, subln_g, lambda_init):
    n_pages = page_table.shape[1]
    past = n_pages * PAGE_SIZE
    tn = sq.shape[1]
    q_pos = past + jnp.arange(tn, dtype=jnp.int32)
    k_pos = jnp.arange(past + tn, dtype=jnp.int32)

    def gather(pool, pages, new):
        rows = pool[pages].reshape((past,) + pool.shape[2:])
        return jnp.concatenate([rows.astype(new.dtype), new], axis=0)[None]

    def one_seq(args):
        pages, sq1, sk1, sv1, dq1, dk1, dv1 = args
        osb = _stick_breaking(sq1[None], gather(pool_sb_k, pages, sk1), gather(pool_sb_v, pages, sv1),
                              q_pos, k_pos)
        odf = _diff_attention(dq1[None], gather(pool_df_k, pages, dk1), gather(pool_df_v, pages, dv1),
                              q_pos, k_pos, lam, subln_g, lambda_init)
        return osb[0], odf[0]

    return lax.map(one_seq, (page_table, sq, sk, sv, dq, dk, dv))


def _mixer_out(osb, odf, ga, gb, w_sb, w_df, w_out):
    B, T = osb.shape[:2]
    ysb = jnp.einsum('bte,ed->btd', osb.reshape(B, T, D_SB), w_sb)
    ydf = jnp.einsum('bte,ed->btd', odf.reshape(B, T, D_DF), w_df)
    merged = jax.nn.sigmoid(ga) * ysb + jax.nn.sigmoid(gb) * ydf
    return jnp.einsum('btd,de->bte', merged, w_out)


def _peer(h, w_q, sub_keys, u, v):
    n = h.shape[0]
    n_pad = -(-n // TOK_BLOCK) * TOK_BLOCK
    hp = jnp.pad(h, ((0, n_pad - n), (0, 0))).reshape(n_pad // TOK_BLOCK, TOK_BLOCK, D_MODEL)

    def block(hb):
        q = jnp.einsum('td,de->te', hb, w_q).reshape(TOK_BLOCK, PEER_HEADS, 2, D_KEY // 2)
        s = jnp.einsum('tpcd,pcnd->tpcn', q.astype(F32), sub_keys.astype(F32))
        v1, i1 = lax.top_k(s[:, :, 0], PEER_TOPK)
        v2, i2 = lax.top_k(s[:, :, 1], PEER_TOPK)
        cand = (v1[..., :, None] + v2[..., None, :]).reshape(TOK_BLOCK, PEER_HEADS, PEER_TOPK * PEER_TOPK)
        cand_idx = (i1[..., :, None] * N_KEYS + i2[..., None, :]).reshape(TOK_BLOCK, PEER_HEADS, PEER_TOPK * PEER_TOPK)
        top_v, top_pos = lax.top_k(cand, PEER_TOPK)
        idx = jnp.take_along_axis(cand_idx, top_pos, axis=-1)
        g = jax.nn.softmax(top_v, axis=-1)
        act = jax.nn.gelu(jnp.einsum('tpkd,td->tpk', u[idx], hb).astype(F32), approximate=False)
        return jnp.einsum('tpk,tpkd->td', (g * act).astype(hb.dtype), v[idx])

    out = lax.map(block, hp).reshape(n_pad, D_MODEL)
    return out[:n]


def _channel_mixer(x, norm_g, w_q, sub_keys, u, v):
    B, T = x.shape[:2]
    h = _rmsnorm(x, norm_g).reshape(B * T, D_MODEL)
    return _peer(h, w_q, sub_keys, u, v).reshape(B, T, D_MODEL)


def setup_inputs(seed: int = 0) -> dict:
    key = jax.random.key(seed)
    ks = jax.random.split(key, 26)
    n_pages = PAST_LEN // PAGE_SIZE
    n_used = DEC_BATCH * n_pages
    n_pool = (n_used * POOL_NUM) // POOL_DEN

    def nrm(k, shape, scale=1.0):
        return jax.random.normal(k, shape, F32) * scale

    def gain(k, shape):
        return 1.0 + 0.05 * nrm(k, shape)

    page_table = jax.random.permutation(ks[6], n_pool)[:n_used].reshape(DEC_BATCH, n_pages).astype(jnp.int32)
    return {
        'x_prompt': nrm(ks[0], (BATCH, SEQ, D_MODEL)),
        'x_sample': nrm(ks[1], (DEC_BATCH, DEC_SEQ, D_MODEL)),
        'cache_sb_k': nrm(ks[2], (DEPTH, n_pool, PAGE_SIZE, H_SB, DH_SB)),
        'cache_sb_v': nrm(ks[3], (DEPTH, n_pool, PAGE_SIZE, H_SB, DH_SB)),
        'cache_df_k': nrm(ks[4], (DEPTH, n_pool, PAGE_SIZE, H_DF, 2, DH_DF)),
        'cache_df_v': nrm(ks[5], (DEPTH, n_pool, PAGE_SIZE, H_DF, 2 * DH_DF)),
        'page_table': page_table,
        'norm_mix_g': gain(ks[7], (DEPTH, D_MODEL)),
        'w_in': nrm(ks[8], (DEPTH, D_MODEL, PROJ_DIM), D_MODEL ** -0.5),
        'df_q_norm_g': gain(ks[9], (DEPTH, DH_DF)),
        'df_k_norm_g': gain(ks[10], (DEPTH, DH_DF)),
        'lambda_q1': nrm(ks[11], (DEPTH, DH_DF), LAMBDA_STD),
        'lambda_k1': nrm(ks[12], (DEPTH, DH_DF), LAMBDA_STD),
        'lambda_q2': nrm(ks[13], (DEPTH, DH_DF), LAMBDA_STD),
        'lambda_k2': nrm(ks[14], (DEPTH, DH_DF), LAMBDA_STD),
        'df_subln_g': gain(ks[15], (DEPTH, 2 * DH_DF)),
        'w_branch_sb': nrm(ks[16], (DEPTH, D_SB, D_MODEL), D_SB ** -0.5),
        'w_branch_df': nrm(ks[17], (DEPTH, D_DF, D_MODEL), D_DF ** -0.5),
        'w_out': nrm(ks[18], (DEPTH, D_MODEL, D_MODEL), D_MODEL ** -0.5),
        'norm_ffn_g': gain(ks[19], (DEPTH, D_MODEL)),
        'peer_w_q': nrm(ks[20], (DEPTH, D_MODEL, PEER_HEADS * D_KEY), D_MODEL ** -0.5),
        'peer_sub_keys': nrm(ks[21], (DEPTH, PEER_HEADS, 2, N_KEYS, D_KEY // 2), (D_KEY // 2) ** -0.5),
        'peer_u': nrm(ks[22], (DEPTH, N_EXPERTS, D_MODEL), D_MODEL ** -0.5),
        'peer_v': nrm(ks[23], (DEPTH, N_EXPERTS, D_MODEL), PEER_HEADS ** -0.5),
    }


def reference(x_prompt, x_sample, cache_sb_k, cache_sb_v, cache_df_k, cache_df_v, page_table,
              norm_mix_g, w_in, df_q_norm_g, df_k_norm_g, lambda_q1, lambda_k1, lambda_q2, lambda_k2,
              df_subln_g, w_branch_sb, w_branch_df, w_out, norm_ffn_g, peer_w_q, peer_sub_keys,
              peer_u, peer_v):
    yp = x_prompt
    ys = x_sample
    p_sbk, p_sbv, p_dfk, p_dfv = [], [], [], []
    s_sbk, s_sbv, s_dfk, s_dfv = [], [], [], []
    for l in range(DEPTH):
        lambda_init = _lambda_init(l)
        lam = (jnp.exp(jnp.sum(lambda_q1[l].astype(F32) * lambda_k1[l].astype(F32)))
               - jnp.exp(jnp.sum(lambda_q2[l].astype(F32) * lambda_k2[l].astype(F32))) + lambda_init)

        sq, sk, sv, dq, dk, dv, ga, gb = _mixer_inputs(yp, norm_mix_g[l], w_in[l], df_q_norm_g[l], df_k_norm_g[l])
        osb, odf = _prompt_attention(sq, sk, sv, dq, dk, dv, lam, df_subln_g[l], lambda_init)
        yp = yp + _mixer_out(osb, odf, ga, gb, w_branch_sb[l], w_branch_df[l], w_out[l])
        p_sbk.append(sk)
        p_sbv.append(sv)
        p_dfk.append(dk)
        p_dfv.append(dv)

        sq, sk, sv, dq, dk, dv, ga, gb = _mixer_inputs(ys, norm_mix_g[l], w_in[l], df_q_norm_g[l], df_k_norm_g[l])
        osb, odf = _sample_attention(sq, sk, sv, dq, dk, dv, cache_sb_k[l], cache_sb_v[l],
                                     cache_df_k[l], cache_df_v[l], page_table, lam, df_subln_g[l], lambda_init)
        ys = ys + _mixer_out(osb, odf, ga, gb, w_branch_sb[l], w_branch_df[l], w_out[l])
        s_sbk.append(sk)
        s_sbv.append(sv)
        s_dfk.append(dk)
        s_dfv.append(dv)

        yp = yp + _channel_mixer(yp, norm_ffn_g[l], peer_w_q[l], peer_sub_keys[l], peer_u[l], peer_v[l])
        ys = ys + _channel_mixer(ys, norm_ffn_g[l], peer_w_q[l], peer_sub_keys[l], peer_u[l], peer_v[l])

    return (yp, ys,
            jnp.stack(p_sbk), jnp.stack(p_sbv), jnp.stack(p_dfk), jnp.stack(p_dfv),
            jnp.stack(s_sbk), jnp.stack(s_sbv), jnp.stack(s_dfk), jnp.stack(s_dfv))
```

```python
import functools
import math

import numpy as np
import jax
import jax.numpy as jnp
from jax import lax
from jax.experimental import pallas as pl
from jax.experimental.pallas import tpu as pltpu

F32 = jnp.float32
BF16 = jnp.bfloat16
EPS = 1e-6
LANES = 128
DH = 128
PAGE = 128
N_KEYS = 128
TOPK = 16
SLAB = 16
VMEM_LIMIT = 48 * 1024 * 1024
NEG = -1e30

_NT = (((1,), (1,)), ((), ()))


def _dot(a, b):
    return jnp.dot(a, b, preferred_element_type=F32)


def _dot_nt(a, b):
    return lax.dot_general(a, b, _NT, preferred_element_type=F32)


def _split_bf16(x):
    hi = x.astype(BF16)
    lo = (x - hi.astype(F32)).astype(BF16)
    return hi, lo


def _dot_split(x, w):
    hi, lo = _split_bf16(x)
    return _dot(hi, w) + _dot(lo, w)


def _softplus(z):
    return jnp.maximum(z, 0.0) + jnp.log1p(jnp.exp(-jnp.abs(z)))


def _params(sem):
    return pltpu.CompilerParams(dimension_semantics=sem, vmem_limit_bytes=VMEM_LIMIT)


def _rms_kernel(x_ref, g_ref, o_ref):
    x = x_ref[...]
    ms = jnp.mean(x * x, axis=-1, keepdims=True)
    o_ref[...] = (x * lax.rsqrt(ms + EPS) * g_ref[...]).astype(o_ref.dtype)


def _rmsnorm(x, g):
    m, d = x.shape
    tm = min(m, 512)
    return pl.pallas_call(
        _rms_kernel,
        grid=(m // tm,),
        in_specs=[pl.BlockSpec((tm, d), lambda i: (i, 0)), pl.BlockSpec((1, d), lambda i: (0, 0))],
        out_specs=pl.BlockSpec((tm, d), lambda i: (i, 0)),
        out_shape=jax.ShapeDtypeStruct((m, d), BF16),
        compiler_params=_params(("parallel",)),
        name="rmsnorm",
    )(x, g.reshape(1, d))


def _proj_kernel(h_ref, w_ref, g_ref, *o_refs, norm, sigmoid):
    acc = _dot(h_ref[...], w_ref[...])
    if norm:
        g = g_ref[...]
        parts = []
        for c in range(acc.shape[1] // DH):
            blk = acc[:, c * DH:(c + 1) * DH]
            ms = jnp.mean(blk * blk, axis=-1, keepdims=True)
            parts.append(blk * lax.rsqrt(ms + EPS) * g)
        acc = jnp.concatenate(parts, axis=1)
    if sigmoid:
        acc = jax.nn.sigmoid(acc)
    for o_ref in o_refs:
        o_ref[...] = acc.astype(o_ref.dtype)


def _proj(h, w, *, dtypes, gain=None, sigmoid=False, name):
    m, k = h.shape
    n = w.shape[1]
    tm = min(m, 512)
    tn = min(n, 1024)
    norm = gain is not None
    g = (gain if norm else jnp.ones((DH,), F32)).reshape(1, DH).astype(F32)
    outs = pl.pallas_call(
        functools.partial(_proj_kernel, norm=norm, sigmoid=sigmoid),
        grid=(m // tm, n // tn),
        in_specs=[pl.BlockSpec((tm, k), lambda i, j: (i, 0)),
                  pl.BlockSpec((k, tn), lambda i, j: (0, j)),
                  pl.BlockSpec((1, DH), lambda i, j: (0, 0))],
        out_specs=[pl.BlockSpec((tm, tn), lambda i, j: (i, j)) for _ in dtypes],
        out_shape=[jax.ShapeDtypeStruct((m, n), dt) for dt in dtypes],
        compiler_params=_params(("parallel", "arbitrary")),
        name=name,
    )(h, w, g)
    return outs


def _lambda_kernel(q1_ref, k1_ref, q2_ref, k2_ref, o_ref, *, lambda_init):
    a = jnp.exp(jnp.sum(q1_ref[...] * k1_ref[...], axis=-1, keepdims=True))
    b = jnp.exp(jnp.sum(q2_ref[...] * k2_ref[...], axis=-1, keepdims=True))
    o_ref[...] = a - b + lambda_init


def _lambda(q1, k1, q2, k2, lambda_init):
    d = q1.shape[0]
    spec = pl.BlockSpec((1, d), lambda: (0, 0))
    out = pl.pallas_call(
        functools.partial(_lambda_kernel, lambda_init=lambda_init),
        in_specs=[spec] * 4,
        out_specs=pl.BlockSpec((1, 1), lambda: (0, 0)),
        out_shape=jax.ShapeDtypeStruct((1, 1), F32),
        name="lambda_scalar",
    )(q1.reshape(1, d), k1.reshape(1, d), q2.reshape(1, d), k2.reshape(1, d))
    return out.reshape(1)


def _sb_prompt_kernel(q_ref, k_ref, v_ref, o_ref, *, tq):
    qi = pl.program_id(2)
    q = q_ref[...]
    row = lax.broadcasted_iota(jnp.int32, (tq, tq), 0)
    col = lax.broadcasted_iota(jnp.int32, (tq, tq), 1)
    later = jnp.where(row > col, 1.0, 0.0).astype(BF16)
    scale = DH ** -0.5

    def body(it, carry):
        acc, c = carry
        kj = qi - it
        start = pl.multiple_of(kj * tq, tq)
        k = k_ref[pl.ds(start, tq), :]
        v = v_ref[pl.ds(start, tq), :]
        z = _dot_nt(q, k) * scale
        mask = (col + kj * tq) < (row + qi * tq)
        sp = _softplus(z)
        lr = jnp.where(mask, -sp, 0.0)
        after = _dot_split(lr, later) + c
        w = jnp.where(mask, jnp.exp(z - sp + after), 0.0)
        acc = acc + _dot(w.astype(BF16), v)
        c = c + jnp.sum(lr, axis=-1, keepdims=True)
        return acc, c

    acc, _ = lax.fori_loop(0, qi + 1, body, (jnp.zeros((tq, DH), F32), jnp.zeros((tq, 1), F32)))
    o_ref[...] = acc.astype(o_ref.dtype)


def _sb_prompt(sq, sk, sv, batch, seq, heads):
    tq = min(seq, 256)
    nq = seq // tq
    return pl.pallas_call(
        functools.partial(_sb_prompt_kernel, tq=tq),
        grid=(batch, heads, nq),
        in_specs=[pl.BlockSpec((tq, DH), lambda b, h, i: (b * nq + i, h)),
                  pl.BlockSpec((seq, DH), lambda b, h, i: (b, h)),
                  pl.BlockSpec((seq, DH), lambda b, h, i: (b, h))],
        out_specs=pl.BlockSpec((tq, DH), lambda b, h, i: (b * nq + i, h)),
        out_shape=jax.ShapeDtypeStruct(sq.shape, BF16),
        compiler_params=_params(("parallel", "parallel", "arbitrary")),
        name="sb_prompt",
    )(sq, sk, sv)


def _df_prompt_kernel(lam_ref, slope_ref, q_ref, k_ref, v_ref, g_ref, o_ref, *, tq, out_scale):
    h = pl.program_id(1)
    qi = pl.program_id(2)
    q = q_ref[...]
    row = lax.broadcasted_iota(jnp.int32, (tq, tq), 0)
    col = lax.broadcasted_iota(jnp.int32, (tq, tq), 1)
    slope = slope_ref[h]
    lam = lam_ref[0]
    scale = DH ** -0.5
    dv = 2 * DH

    def body(kj, carry):
        start = pl.multiple_of(kj * tq, tq)
        k = k_ref[pl.ds(start, tq), :]
        v = v_ref[pl.ds(start, tq), :]
        dist = (row + qi * tq) - (col + kj * tq)
        bias = slope * dist.astype(F32)
        mask = dist >= 0
        new = []
        for c in range(2):
            m, l, acc = carry[3 * c:3 * c + 3]
            s = _dot_nt(q[:, c * DH:(c + 1) * DH], k[:, c * DH:(c + 1) * DH]) * scale - bias
            s = jnp.where(mask, s, NEG)
            m_new = jnp.maximum(m, jnp.max(s, axis=-1, keepdims=True))
            alpha = jnp.exp(m - m_new)
            p = jnp.exp(s - m_new)
            l = l * alpha + jnp.sum(p, axis=-1, keepdims=True)
            acc = acc * alpha + _dot(p.astype(BF16), v)
            new += [m_new, l, acc]
        return tuple(new)

    init = (jnp.full((tq, 1), NEG, F32), jnp.zeros((tq, 1), F32), jnp.zeros((tq, dv), F32)) * 2
    m0, l0, a0, m1, l1, a1 = lax.fori_loop(0, qi + 1, body, init)
    o = a0 / l0 - lam * (a1 / l1)
    ms = jnp.mean(o * o, axis=-1, keepdims=True)
    o_ref[...] = (o * lax.rsqrt(ms + EPS) * g_ref[...] * out_scale).astype(o_ref.dtype)


def _df_prompt(lam, slopes, dq, dk, dv, subln_g, batch, seq, heads, out_scale):
    tq = min(seq, 256)
    nq = seq // tq
    w = 2 * DH
    smem = pl.BlockSpec(memory_space=pltpu.SMEM)
    return pl.pallas_call(
        functools.partial(_df_prompt_kernel, tq=tq, out_scale=out_scale),
        grid=(batch, heads, nq),
        in_specs=[smem, smem,
                  pl.BlockSpec((tq, w), lambda b, h, i: (b * nq + i, h)),
                  pl.BlockSpec((seq, w), lambda b, h, i: (b, h)),
                  pl.BlockSpec((seq, w), lambda b, h, i: (b, h)),
                  pl.BlockSpec((1, w), lambda b, h, i: (0, 0))],
        out_specs=pl.BlockSpec((tq, w), lambda b, h, i: (b * nq + i, h)),
        out_shape=jax.ShapeDtypeStruct(dq.shape, BF16),
        compiler_params=_params(("parallel", "parallel", "arbitrary")),
        name="df_prompt",
    )(lam, slopes, dq, dk, dv, subln_g.reshape(1, w))


def _sample_kernel(pt_ref, lam_ref, slope_ref,
                   sq_ref, dq_ref, g_ref,
                   nsk_ref, nsv_ref, ndk_ref, ndv_ref,
                   psk_ref, psv_ref, pdk_ref, pdv_ref,
                   osb_ref, odf_ref,
                   qsb_s, qdf_s, c_s, asb_s, m_s, l_s, adf_s,
                   *, tn, n_pages, h_sb, h_df, out_scale):
    del pt_ref
    p = pl.program_id(1)
    d_sb = h_sb * DH
    d_df = h_df * 2 * DH
    rows = tn * 8
    assert h_sb == 8 and h_df * 2 == 8
    past = n_pages * PAGE
    scale = DH ** -0.5
    rid = lax.broadcasted_iota(jnp.int32, (rows, PAGE), 0)
    lane = lax.broadcasted_iota(jnp.int32, (rows, PAGE), 1)
    tok = rid // 8
    grp = lax.broadcasted_iota(jnp.int32, (8, d_sb), 0)
    colblk = lax.broadcasted_iota(jnp.int32, (8, d_sb), 1) // DH
    blockdiag = grp == colblk
    r8 = lax.broadcasted_iota(jnp.int32, (PAGE, PAGE), 0)
    c8 = lax.broadcasted_iota(jnp.int32, (PAGE, PAGE), 1)
    later = jnp.where(r8 > c8, 1.0, 0.0).astype(BF16)

    def slope_rows():
        hid = (lax.broadcasted_iota(jnp.int32, (rows, 1), 0) % 8) // 2
        s = jnp.zeros((rows, 1), F32)
        for hh in range(h_df):
            s = jnp.where(hid == hh, slope_ref[hh], s)
        return s

    def process(ksb, vsb, kdf, vdf, kpos, sb_mask, df_mask):
        z = _dot_nt(qsb_s[...], ksb.astype(BF16)) * scale
        sp = _softplus(z)
        lr = -sp if sb_mask is None else jnp.where(sb_mask, -sp, 0.0)
        after = _dot_split(lr, later) + c_s[...]
        w = jnp.exp(z - sp + after)
        if sb_mask is not None:
            w = jnp.where(sb_mask, w, 0.0)
        asb_s[...] += _dot(w.astype(BF16), vsb.astype(BF16))
        c_s[...] += jnp.sum(lr, axis=-1, keepdims=True)
        dist = (past + tok - kpos).astype(F32)
        s = _dot_nt(qdf_s[...], kdf.astype(BF16)) * scale - slope_rows() * dist
        if df_mask is not None:
            s = jnp.where(df_mask, s, NEG)
        m_old = m_s[...]
        m_new = jnp.maximum(m_old, jnp.max(s, axis=-1, keepdims=True))
        alpha = jnp.exp(m_old - m_new)
        pr = jnp.exp(s - m_new)
        l_s[...] = l_s[...] * alpha + jnp.sum(pr, axis=-1, keepdims=True)
        adf_s[...] = adf_s[...] * alpha + _dot(pr.astype(BF16), vdf.astype(BF16))
        m_s[...] = m_new

    @pl.when(p == 0)
    def _():
        for t in range(tn):
            qsb_s[t * 8:(t + 1) * 8, :] = jnp.where(blockdiag, sq_ref[0, t:t + 1, :], 0.0).astype(BF16)
            qdf_s[t * 8:(t + 1) * 8, :] = jnp.where(blockdiag, dq_ref[0, t:t + 1, :], 0.0).astype(BF16)
        c_s[...] = jnp.zeros_like(c_s)
        asb_s[...] = jnp.zeros_like(asb_s)
        m_s[...] = jnp.full_like(m_s, NEG)
        l_s[...] = jnp.zeros_like(l_s)
        adf_s[...] = jnp.zeros_like(adf_s)
        kpos = past + lane
        process(nsk_ref[0], nsv_ref[0], ndk_ref[0], ndv_ref[0], kpos,
                lane < tok, lane <= tok)

    @pl.when(p > 0)
    def _():
        kpos = (n_pages - p) * PAGE + lane
        process(psk_ref[0], psv_ref[0], pdk_ref[0], pdv_ref[0], kpos, None, None)

    @pl.when(p == n_pages)
    def _():
        lam = lam_ref[0]
        dmap = (grp // 2 == lax.broadcasted_iota(jnp.int32, (8, d_df), 1) // (2 * DH))
        an = adf_s[...] / l_s[...]
        for t in range(tn):
            tile = asb_s[t * 8:(t + 1) * 8, :]
            osb_ref[0, t:t + 1, :] = jnp.sum(jnp.where(blockdiag, tile, 0.0), axis=0,
                                            keepdims=True).astype(osb_ref.dtype)
            dt = an[t * 8:(t + 1) * 8, :]
            o0 = jnp.sum(jnp.where(dmap & (grp % 2 == 0), dt, 0.0), axis=0, keepdims=True)
            o1 = jnp.sum(jnp.where(dmap & (grp % 2 == 1), dt, 0.0), axis=0, keepdims=True)
            o = o0 - lam * o1
            parts = []
            for hh in range(h_df):
                blk = o[:, hh * 2 * DH:(hh + 1) * 2 * DH]
                ms = jnp.mean(blk * blk, axis=-1, keepdims=True)
                parts.append(blk * lax.rsqrt(ms + EPS) * g_ref[...] * out_scale)
            odf_ref[0, t:t + 1, :] = jnp.concatenate(parts, axis=1).astype(odf_ref.dtype)


def _sample_attention(page_table, lam, slopes, sq, dq, subln_g, new_kv, pools, *, tn, h_sb, h_df, out_scale):
    b, n_pages = page_table.shape
    d = sq.shape[-1]
    rows = tn * 8
    smem = pl.BlockSpec(memory_space=pltpu.SMEM)
    qspec = pl.BlockSpec((1, tn, d), lambda i, p, pt: (i, 0, 0))
    nspec = pl.BlockSpec((1, PAGE, d), lambda i, p, pt: (i, 0, 0))
    pspec = pl.BlockSpec((1, PAGE, d), lambda i, p, pt: (pt[i, n_pages - jnp.maximum(p, 1)], 0, 0))
    grid_spec = pltpu.PrefetchScalarGridSpec(
        num_scalar_prefetch=1,
        grid=(b, n_pages + 1),
        in_specs=[smem, smem, qspec, qspec, pl.BlockSpec((1, 2 * DH), lambda i, p, pt: (0, 0)),
                  nspec, nspec, nspec, nspec, pspec, pspec, pspec, pspec],
        out_specs=[qspec, qspec],
        scratch_shapes=[pltpu.VMEM((rows, d), BF16), pltpu.VMEM((rows, d), BF16),
                        pltpu.VMEM((rows, 1), F32), pltpu.VMEM((rows, d), F32),
                        pltpu.VMEM((rows, 1), F32), pltpu.VMEM((rows, 1), F32),
                        pltpu.VMEM((rows, d), F32)],
    )
    return pl.pallas_call(
        functools.partial(_sample_kernel, tn=tn, n_pages=n_pages, h_sb=h_sb, h_df=h_df,
                          out_scale=out_scale),
        grid_spec=grid_spec,
        out_shape=[jax.ShapeDtypeStruct((b, tn, d), F32)] * 2,
        compiler_params=_params(("parallel", "arbitrary")),
        name="sample_attention",
    )(page_table, lam, slopes, sq, dq, subln_g.reshape(1, 2 * DH), *new_kv, *pools)


def _merge_kernel(osb_ref, odf_ref, wsb_ref, wdf_ref, ga_ref, gb_ref, o_ref):
    ysb = _dot(osb_ref[...], wsb_ref[...])
    ydf = _dot(odf_ref[...], wdf_ref[...])
    o_ref[...] = (ga_ref[...].astype(F32) * ysb + gb_ref[...].astype(F32) * ydf).astype(o_ref.dtype)


def _merge(osb, odf, w_sb, w_df, gates):
    m, k = osb.shape
    n = w_sb.shape[1]
    tm = min(m, 512)
    tn = min(n, 1024)
    nj = n // tn
    return pl.pallas_call(
        _merge_kernel,
        grid=(m // tm, nj),
        in_specs=[pl.BlockSpec((tm, k), lambda i, j: (i, 0)),
                  pl.BlockSpec((tm, k), lambda i, j: (i, 0)),
                  pl.BlockSpec((k, tn), lambda i, j: (0, j)),
                  pl.BlockSpec((k, tn), lambda i, j: (0, j)),
                  pl.BlockSpec((tm, tn), lambda i, j: (i, j)),
                  pl.BlockSpec((tm, tn), lambda i, j: (i, j + nj))],
        out_specs=pl.BlockSpec((tm, tn), lambda i, j: (i, j)),
        out_shape=jax.ShapeDtypeStruct((m, n), BF16),
        compiler_params=_params(("parallel", "arbitrary")),
        name="mixer_merge",
    )(osb, odf, w_sb, w_df, gates, gates)


def _resid_kernel(a_ref, w_ref, x_ref, o_ref):
    o_ref[...] = x_ref[...] + _dot(a_ref[...], w_ref[...])


def _resid_matmul(a, w, x):
    m, k = a.shape
    n = w.shape[1]
    tm = min(m, 512)
    tn = min(n, 1024)
    return pl.pallas_call(
        _resid_kernel,
        grid=(m // tm, n // tn),
        in_specs=[pl.BlockSpec((tm, k), lambda i, j: (i, 0)),
                  pl.BlockSpec((k, tn), lambda i, j: (0, j)),
                  pl.BlockSpec((tm, tn), lambda i, j: (i, j))],
        out_specs=pl.BlockSpec((tm, tn), lambda i, j: (i, j)),
        out_shape=jax.ShapeDtypeStruct((m, n), F32),
        compiler_params=_params(("parallel", "arbitrary")),
        name="mixer_out",
    )(a, w, x)


def _extract_top(vals, key, payload, big):
    m = jnp.max(vals, axis=0, keepdims=True)
    kmin = jnp.min(jnp.where(vals == m, key, big), axis=0, keepdims=True)
    sel = key == kmin
    pay = jnp.max(jnp.where(sel, payload, -1), axis=0, keepdims=True)
    return m, pay, jnp.where(sel, -jnp.inf, vals)


def _peer_route_kernel(y_ref, g_ref, wq_ref, keys_ref, h_ref, idx_ref, gate_ref,
                       s_s, v_s, i_s, *, heads):
    tt = y_ref.shape[0]
    halves = tt // LANES
    y = y_ref[...]
    ms = jnp.mean(y * y, axis=-1, keepdims=True)
    h = (y * lax.rsqrt(ms + EPS) * g_ref[...]).astype(BF16)
    h_ref[...] = h
    q = _dot(h, wq_ref[...]).astype(BF16)
    n_pc = 2 * heads
    for pc in range(n_pc):
        s = _dot_nt(keys_ref[pc], q[:, pc * DH:(pc + 1) * DH])
        for hf in range(halves):
            s_s[hf * n_pc + pc] = s[:, hf * LANES:(hf + 1) * LANES]

    key_iota = lax.broadcasted_iota(jnp.int32, (N_KEYS, LANES), 0)

    def stage1(j, _):
        vals = s_s[j]
        for a in range(TOPK):
            m, pay, vals = _extract_top(vals, key_iota, key_iota, N_KEYS)
            v_s[j, a:a + 1, :] = m
            i_s[j, a:a + 1, :] = pay
        return 0

    lax.fori_loop(0, halves * n_pc, stage1, 0)

    sub = lax.broadcasted_iota(jnp.int32, (8, LANES), 0)
    pos = jnp.concatenate([sub, sub + 8] + [sub + a * TOPK for a in range(1, 8)]
                          + [(sub + 8) * TOPK], axis=0)

    def stage2(j, _):
        hf = j // heads
        p = j % heads
        j1 = hf * n_pc + 2 * p
        v1, v2 = v_s[j1], v_s[j1 + 1]
        i1, i2 = i_s[j1] * N_KEYS, i_s[j1 + 1]
        cand = jnp.concatenate(
            [v1[0:1] + v2[0:8], v1[0:1] + v2[8:16]]
            + [v1[a:a + 1] + v2[0:8] for a in range(1, 8)] + [v1[8:16] + v2[0:1]], axis=0)
        eidx = jnp.concatenate(
            [i1[0:1] + i2[0:8], i1[0:1] + i2[8:16]]
            + [i1[a:a + 1] + i2[0:8] for a in range(1, 8)] + [i1[8:16] + i2[0:1]], axis=0)
        tops, ids = [], []
        for _k in range(TOPK):
            m, pay, cand = _extract_top(cand, pos, eidx, TOPK * TOPK)
            tops.append(m)
            ids.append(pay)
        tv = jnp.concatenate(tops, axis=0)
        ex = jnp.exp(tv - tops[0])
        gate = ex / jnp.sum(ex, axis=0, keepdims=True)
        r0 = pl.multiple_of(p * TOPK, TOPK)
        for hh in range(halves):
            @pl.when(hf == hh)
            def _():
                idx_ref[pl.ds(r0, TOPK), hh * LANES:(hh + 1) * LANES] = jnp.concatenate(ids, axis=0)
                gate_ref[pl.ds(r0, TOPK), hh * LANES:(hh + 1) * LANES] = gate
        return 0

    lax.fori_loop(0, halves * heads, stage2, 0)


def _peer_route(y, norm_g, w_q, keys):
    n, d = y.shape
    n_pc = keys.shape[0]
    heads = n_pc // 2
    tt = min(n, 256)
    halves = tt // LANES
    rows = heads * TOPK
    return pl.pallas_call(
        functools.partial(_peer_route_kernel, heads=heads),
        grid=(n // tt,),
        in_specs=[pl.BlockSpec((tt, d), lambda i: (i, 0)),
                  pl.BlockSpec((1, d), lambda i: (0, 0)),
                  pl.BlockSpec(w_q.shape, lambda i: (0, 0)),
                  pl.BlockSpec(keys.shape, lambda i: (0, 0, 0))],
        out_specs=[pl.BlockSpec((tt, d), lambda i: (i, 0)),
                   pl.BlockSpec((rows, tt), lambda i: (0, i)),
                   pl.BlockSpec((rows, tt), lambda i: (0, i))],
        out_shape=[jax.ShapeDtypeStruct((n, d), BF16),
                   jax.ShapeDtypeStruct((rows, n), jnp.int32),
                   jax.ShapeDtypeStruct((rows, n), F32)],
        scratch_shapes=[pltpu.VMEM((halves * n_pc, N_KEYS, LANES), F32),
                        pltpu.VMEM((halves * n_pc, TOPK, LANES), F32),
                        pltpu.VMEM((halves * n_pc, TOPK, LANES), jnp.int32)],
        compiler_params=_params(("parallel",)),
        name="peer_route",
    )(y, norm_g.reshape(1, d), w_q, keys)


def _peer_expert_kernel(idx0_ref, idxn_ref, h_ref, gate_ref, y_ref, diag_ref, seg_ref, segt_ref,
                        uv_ref, o_ref, buf, sem, *, tb, picks):
    i = pl.program_id(0)
    n = pl.num_programs(0)
    per_step = tb * picks

    def issue(idx_ref, slot):
        def body(r, _):
            e = idx_ref[r // picks, r % picks]
            pltpu.make_async_copy(uv_ref.at[e], buf.at[slot, r], sem.at[slot]).start()
            return 0
        lax.fori_loop(0, per_step, body, 0)

    @pl.when(i == 0)
    def _():
        issue(idx0_ref, 0)

    @pl.when(i + 1 < n)
    def _():
        issue(idxn_ref, (i + 1) % 2)

    slot = i % 2
    pltpu.make_async_copy(buf.at[slot], buf.at[slot], sem.at[slot]).wait()

    diag = diag_ref[...]
    rm = []
    for t in range(tb):
        ub = buf[slot, pl.ds(t * picks, picks), 0:SLAB, :].reshape(picks * SLAB, LANES)
        rm.append(_dot_nt(h_ref[t], ub) * diag)
    part = _dot_split(jnp.concatenate(rm, axis=0), seg_ref[...])
    act = jnp.sum(part.reshape(tb, SLAB, picks), axis=1)
    gelu = 0.5 * act * (1.0 + lax.erf(act * math.sqrt(0.5)))
    coef = gate_ref[...] * gelu
    cx = _dot_split(coef, segt_ref[...])
    for t in range(tb):
        vb = buf[slot, pl.ds(t * picks, picks), SLAB:2 * SLAB, :].reshape(picks * SLAB, LANES)
        e = (diag * cx[t:t + 1, :]).astype(BF16)
        o_ref[t] = y_ref[t] + _dot(e, vb)


def _peer_experts(idx, gate, h_slab, y_slab, uv):
    n, picks = idx.shape
    tb = 8
    nb = n // tb
    j = np.arange(picks * SLAB)
    diag = jnp.asarray((j[None, :] % SLAB == np.arange(SLAB)[:, None]).astype(np.float32))
    seg_np = (j[:, None] // SLAB == np.arange(picks)[None, :]).astype(np.float32)
    seg = jnp.asarray(seg_np, dtype=BF16)
    segt = jnp.asarray(seg_np.T, dtype=BF16)
    const2 = lambda i: (0, 0)
    tok3 = lambda i: (i, 0, 0)
    return pl.pallas_call(
        functools.partial(_peer_expert_kernel, tb=tb, picks=picks),
        grid=(nb,),
        in_specs=[pl.BlockSpec((tb, picks), lambda i: (0, 0), memory_space=pltpu.SMEM),
                  pl.BlockSpec((tb, picks), lambda i: (jnp.minimum(i + 1, nb - 1), 0),
                               memory_space=pltpu.SMEM),
                  pl.BlockSpec((tb, SLAB, LANES), tok3),
                  pl.BlockSpec((tb, picks), lambda i: (i, 0)),
                  pl.BlockSpec((tb, SLAB, LANES), tok3),
                  pl.BlockSpec(diag.shape, const2),
                  pl.BlockSpec(seg.shape, const2),
                  pl.BlockSpec(segt.shape, const2),
                  pl.BlockSpec(memory_space=pl.ANY)],
        out_specs=pl.BlockSpec((tb, SLAB, LANES), tok3),
        out_shape=jax.ShapeDtypeStruct(y_slab.shape, F32),
        scratch_shapes=[pltpu.VMEM((2, tb * picks, 2 * SLAB, LANES), BF16),
                        pltpu.SemaphoreType.DMA((2,))],
        compiler_params=_params(("arbitrary",)),
        name="peer_experts",
    )(idx, idx, h_slab, gate, y_slab, diag, seg, segt, uv)


def _channel_mixer(y, norm_g, w_q, keys, uv):
    n, d = y.shape
    h, idx_t, gate_t = _peer_route(y, norm_g, w_q, keys)
    out = _peer_experts(idx_t.T, gate_t.T, h.reshape(n, SLAB, LANES), y.reshape(n, SLAB, LANES), uv)
    return out.reshape(n, d)


def kernel(x_prompt, x_sample, cache_sb_k, cache_sb_v, cache_df_k, cache_df_v, page_table, norm_mix_g, w_in, df_q_norm_g, df_k_norm_g, lambda_q1, lambda_k1, lambda_q2, lambda_k2, df_subln_g, w_branch_sb, w_branch_df, w_out, norm_ffn_g, peer_w_q, peer_sub_keys, peer_u, peer_v):
    batch, seq, d_model = x_prompt.shape
    dec_batch, dec_seq, _ = x_sample.shape
    depth = w_in.shape[0]
    h_sb = cache_sb_k.shape[3]
    h_df = cache_df_k.shape[3]
    d_sb = h_sb * DH
    d_df = h_df * 2 * DH
    n_pool = cache_sb_k.shape[1]
    heads = peer_sub_keys.shape[1]
    assert d_model == SLAB * LANES and cache_sb_k.shape[2] == PAGE and peer_sub_keys.shape[3] == N_KEYS
    slopes = jnp.asarray(2.0 ** (-8.0 * np.arange(1, h_df + 1) / h_df), F32)

    yp = x_prompt.reshape(batch * seq, d_model)
    ys = x_sample.reshape(dec_batch * dec_seq, d_model)
    outs = [[] for _ in range(8)]
    for l in range(depth):
        lambda_init = 0.8 - 0.6 * math.exp(-0.3 * l)
        out_scale = 1.0 - lambda_init
        lam = _lambda(lambda_q1[l], lambda_k1[l], lambda_q2[l], lambda_k2[l], lambda_init)
        wl = w_in[l].astype(BF16)
        cuts = np.cumsum([0, d_sb, d_sb, d_sb, d_df, d_df, d_df, 2 * d_model])
        w_sq, w_sk, w_sv, w_dq, w_dk, w_dv, w_g = [wl[:, a:b] for a, b in zip(cuts[:-1], cuts[1:])]
        w_sb = w_branch_sb[l].astype(BF16)
        w_df = w_branch_df[l].astype(BF16)
        w_o = w_out[l].astype(BF16)
        w_q = peer_w_q[l].astype(BF16)
        keys = peer_sub_keys[l].reshape(2 * heads, N_KEYS, DH).astype(BF16)
        n_exp = peer_u.shape[1]
        uv = jnp.concatenate([peer_u[l].astype(BF16).reshape(n_exp, SLAB, LANES),
                              peer_v[l].astype(BF16).reshape(n_exp, SLAB, LANES)], axis=1)

        def mixer_inputs(x, tag):
            h = _rmsnorm(x, norm_mix_g[l])
            (sq,) = _proj(h, w_sq, dtypes=(BF16,), name=f"proj_sq_{tag}")
            sk32, sk = _proj(h, w_sk, dtypes=(F32, BF16), name=f"proj_sk_{tag}")
            sv32, sv = _proj(h, w_sv, dtypes=(F32, BF16), name=f"proj_sv_{tag}")
            (dq,) = _proj(h, w_dq, dtypes=(BF16,), gain=df_q_norm_g[l], name=f"proj_dq_{tag}")
            dk32, dk = _proj(h, w_dk, dtypes=(F32, BF16), gain=df_k_norm_g[l], name=f"proj_dk_{tag}")
            dv32, dv = _proj(h, w_dv, dtypes=(F32, BF16), name=f"proj_dv_{tag}")
            (gates,) = _proj(h, w_g, dtypes=(BF16,), sigmoid=True, name=f"proj_gates_{tag}")
            return (sq, sk, sv, dq, dk, dv, gates), (sk32, sv32, dk32, dv32)

        def mixer_out(x, osb, odf, gates):
            merged = _merge(osb, odf, w_sb, w_df, gates)
            return _resid_matmul(merged, w_o, x)

        (sq, sk, sv, dq, dk, dv, gates), kv32 = mixer_inputs(yp, "p")
        osb = _sb_prompt(sq, sk, sv, batch, seq, h_sb)
        odf = _df_prompt(lam, slopes, dq, dk, dv, df_subln_g[l], batch, seq, h_df, out_scale)
        yp = mixer_out(yp, osb, odf, gates)
        for o, a in zip(outs[:4], kv32):
            o.append(a)

        hs = _rmsnorm(ys, norm_mix_g[l])
        (sq32,) = _proj(hs, w_sq, dtypes=(F32,), name="proj_sq_s")
        (sk32,) = _proj(hs, w_sk, dtypes=(F32,), name="proj_sk_s")
        (sv32,) = _proj(hs, w_sv, dtypes=(F32,), name="proj_sv_s")
        (dq32,) = _proj(hs, w_dq, dtypes=(F32,), gain=df_q_norm_g[l], name="proj_dq_s")
        (dk32,) = _proj(hs, w_dk, dtypes=(F32,), gain=df_k_norm_g[l], name="proj_dk_s")
        (dv32,) = _proj(hs, w_dv, dtypes=(F32,), name="proj_dv_s")
        (gates_s,) = _proj(hs, w_g, dtypes=(BF16,), sigmoid=True, name="proj_gates_s")

        def padded(a):
            a = a.reshape(dec_batch, dec_seq, a.shape[-1])
            return jnp.pad(a, ((0, 0), (0, PAGE - dec_seq), (0, 0)))

        pools = (cache_sb_k[l].reshape(n_pool, PAGE, d_sb), cache_sb_v[l].reshape(n_pool, PAGE, d_sb),
                 cache_df_k[l].reshape(n_pool, PAGE, d_df), cache_df_v[l].reshape(n_pool, PAGE, d_df))
        osb_s, odf_s = _sample_attention(
            page_table, lam, slopes,
            sq32.reshape(dec_batch, dec_seq, d_sb), dq32.reshape(dec_batch, dec_seq, d_df),
            df_subln_g[l], tuple(padded(a) for a in (sk32, sv32, dk32, dv32)), pools,
            tn=dec_seq, h_sb=h_sb, h_df=h_df, out_scale=out_scale)
        ys = mixer_out(ys, osb_s.reshape(-1, d_sb).astype(BF16), odf_s.reshape(-1, d_df).astype(BF16),
                       gates_s)
        for o, a in zip(outs[4:], (sk32, sv32, dk32, dv32)):
            o.append(a)

        yp = _channel_mixer(yp, norm_ffn_g[l], w_q, keys, uv)
        ys = _channel_mixer(ys, norm_ffn_g[l], w_q, keys, uv)

    def stack(items, lead, tail):
        return jnp.stack(items).reshape((depth,) + lead + tail)

    pl_, sl_ = (batch, seq), (dec_batch, dec_seq)
    return (yp.reshape(batch, seq, d_model), ys.reshape(dec_batch, dec_seq, d_model),
            stack(outs[0], pl_, (h_sb, DH)), stack(outs[1], pl_, (h_sb, DH)),
            stack(outs[2], pl_, (h_df, 2, DH)), stack(outs[3], pl_, (h_df, 2 * DH)),
            stack(outs[4], sl_, (h_sb, DH)), stack(outs[5], sl_, (h_sb, DH)),
            stack(outs[6], sl_, (h_df, 2, DH)), stack(outs[7], sl_, (h_df, 2 * DH)))
```

```python
import functools
import math

import numpy as np
import jax
import jax.numpy as jnp
from jax import lax
from jax.experimental import pallas as pl
from jax.experimental.pallas import tpu as pltpu

F32 = jnp.float32
BF16 = jnp.bfloat16
EPS = 1e-6
LANES = 128
DH = 128
PAGE = 128
N_KEYS = 128
TOPK = 16
SLAB = 16
ISSUE_UNROLL = 8
NSLOT = 3
AHEAD = NSLOT - 1
VMEM_LIMIT = 48 * 1024 * 1024
NEG = -1e30

_NT = (((1,), (1,)), ((), ()))


def _dot(a, b):
    return jnp.dot(a, b, preferred_element_type=F32)


def _dot_nt(a, b):
    return lax.dot_general(a, b, _NT, preferred_element_type=F32)


def _split_bf16(x):
    hi = x.astype(BF16)
    lo = (x - hi.astype(F32)).astype(BF16)
    return hi, lo


def _dot_split(x, w):
    hi, lo = _split_bf16(x)
    return _dot(hi, w) + _dot(lo, w)


def _softplus(z):
    return jnp.maximum(z, 0.0) + jnp.log1p(jnp.exp(-jnp.abs(z)))


def _params(sem):
    return pltpu.CompilerParams(dimension_semantics=sem, vmem_limit_bytes=VMEM_LIMIT)


def _rms_kernel(x_ref, g_ref, o_ref):
    x = x_ref[...]
    ms = jnp.mean(x * x, axis=-1, keepdims=True)
    o_ref[...] = (x * lax.rsqrt(ms + EPS) * g_ref[...]).astype(o_ref.dtype)


def _rmsnorm(x, g):
    m, d = x.shape
    tm = min(m, 512)
    return pl.pallas_call(
        _rms_kernel,
        grid=(m // tm,),
        in_specs=[pl.BlockSpec((tm, d), lambda i: (i, 0)), pl.BlockSpec((1, d), lambda i: (0, 0))],
        out_specs=pl.BlockSpec((tm, d), lambda i: (i, 0)),
        out_shape=jax.ShapeDtypeStruct((m, d), BF16),
        compiler_params=_params(("parallel",)),
        name="rmsnorm",
    )(x, g.reshape(1, d))


def _proj_kernel(h_ref, w_ref, g_ref, *o_refs, norm, sigmoid):
    acc = _dot(h_ref[...], w_ref[...])
    if norm:
        g = g_ref[...]
        parts = []
        for c in range(acc.shape[1] // DH):
            blk = acc[:, c * DH:(c + 1) * DH]
            ms = jnp.mean(blk * blk, axis=-1, keepdims=True)
            parts.append(blk * lax.rsqrt(ms + EPS) * g)
        acc = jnp.concatenate(parts, axis=1)
    if sigmoid:
        acc = jax.nn.sigmoid(acc)
    for o_ref in o_refs:
        o_ref[...] = acc.astype(o_ref.dtype)


def _proj(h, w, *, dtypes, gain=None, sigmoid=False, name):
    m, k = h.shape
    n = w.shape[1]
    tm = min(m, 512)
    tn = min(n, 1024)
    norm = gain is not None
    g = (gain if norm else jnp.ones((DH,), F32)).reshape(1, DH).astype(F32)
    outs = pl.pallas_call(
        functools.partial(_proj_kernel, norm=norm, sigmoid=sigmoid),
        grid=(m // tm, n // tn),
        in_specs=[pl.BlockSpec((tm, k), lambda i, j: (i, 0)),
                  pl.BlockSpec((k, tn), lambda i, j: (0, j)),
                  pl.BlockSpec((1, DH), lambda i, j: (0, 0))],
        out_specs=[pl.BlockSpec((tm, tn), lambda i, j: (i, j)) for _ in dtypes],
        out_shape=[jax.ShapeDtypeStruct((m, n), dt) for dt in dtypes],
        compiler_params=_params(("parallel", "arbitrary")),
        name=name,
    )(h, w, g)
    return outs


def _lambda_kernel(q1_ref, k1_ref, q2_ref, k2_ref, o_ref, *, lambda_init):
    a = jnp.exp(jnp.sum(q1_ref[...] * k1_ref[...], axis=-1, keepdims=True))
    b = jnp.exp(jnp.sum(q2_ref[...] * k2_ref[...], axis=-1, keepdims=True))
    o_ref[...] = a - b + lambda_init


def _lambda(q1, k1, q2, k2, lambda_init):
    d = q1.shape[0]
    spec = pl.BlockSpec((1, d), lambda: (0, 0))
    out = pl.pallas_call(
        functools.partial(_lambda_kernel, lambda_init=lambda_init),
        in_specs=[spec] * 4,
        out_specs=pl.BlockSpec((1, 1), lambda: (0, 0)),
        out_shape=jax.ShapeDtypeStruct((1, 1), F32),
        name="lambda_scalar",
    )(q1.reshape(1, d), k1.reshape(1, d), q2.reshape(1, d), k2.reshape(1, d))
    return out.reshape(1)


def _sb_prompt_kernel(q_ref, k_ref, v_ref, o_ref, *, tq):
    qi = pl.program_id(2)
    q = q_ref[...]
    row = lax.broadcasted_iota(jnp.int32, (tq, tq), 0)
    col = lax.broadcasted_iota(jnp.int32, (tq, tq), 1)
    later = jnp.where(row > col, 1.0, 0.0).astype(BF16)
    scale = DH ** -0.5

    def body(it, carry):
        acc, c = carry
        kj = qi - it
        start = pl.multiple_of(kj * tq, tq)
        k = k_ref[pl.ds(start, tq), :]
        v = v_ref[pl.ds(start, tq), :]
        z = _dot_nt(q, k) * scale
        mask = (col + kj * tq) < (row + qi * tq)
        sp = _softplus(z)
        lr = jnp.where(mask, -sp, 0.0)
        after = _dot_split(lr, later) + c
        w = jnp.where(mask, jnp.exp(z - sp + after), 0.0)
        acc = acc + _dot(w.astype(BF16), v)
        c = c + jnp.sum(lr, axis=-1, keepdims=True)
        return acc, c

    acc, _ = lax.fori_loop(0, qi + 1, body, (jnp.zeros((tq, DH), F32), jnp.zeros((tq, 1), F32)))
    o_ref[...] = acc.astype(o_ref.dtype)


def _sb_prompt(sq, sk, sv, batch, seq, heads):
    tq = min(seq, 256)
    nq = seq // tq
    return pl.pallas_call(
        functools.partial(_sb_prompt_kernel, tq=tq),
        grid=(batch, heads, nq),
        in_specs=[pl.BlockSpec((tq, DH), lambda b, h, i: (b * nq + i, h)),
                  pl.BlockSpec((seq, DH), lambda b, h, i: (b, h)),
                  pl.BlockSpec((seq, DH), lambda b, h, i: (b, h))],
        out_specs=pl.BlockSpec((tq, DH), lambda b, h, i: (b * nq + i, h)),
        out_shape=jax.ShapeDtypeStruct(sq.shape, BF16),
        compiler_params=_params(("parallel", "parallel", "arbitrary")),
        name="sb_prompt",
    )(sq, sk, sv)


def _df_prompt_kernel(lam_ref, slope_ref, q_ref, k_ref, v_ref, g_ref, o_ref, *, tq, out_scale):
    h = pl.program_id(1)
    qi = pl.program_id(2)
    q = q_ref[...]
    row = lax.broadcasted_iota(jnp.int32, (tq, tq), 0)
    col = lax.broadcasted_iota(jnp.int32, (tq, tq), 1)
    slope = slope_ref[h]
    lam = lam_ref[0]
    scale = DH ** -0.5
    dv = 2 * DH

    def body(kj, carry):
        start = pl.multiple_of(kj * tq, tq)
        k = k_ref[pl.ds(start, tq), :]
        v = v_ref[pl.ds(start, tq), :]
        dist = (row + qi * tq) - (col + kj * tq)
        bias = slope * dist.astype(F32)
        mask = dist >= 0
        new = []
        for c in range(2):
            m, l, acc = carry[3 * c:3 * c + 3]
            s = _dot_nt(q[:, c * DH:(c + 1) * DH], k[:, c * DH:(c + 1) * DH]) * scale - bias
            s = jnp.where(mask, s, NEG)
            m_new = jnp.maximum(m, jnp.max(s, axis=-1, keepdims=True))
            alpha = jnp.exp(m - m_new)
            p = jnp.exp(s - m_new)
            l = l * alpha + jnp.sum(p, axis=-1, keepdims=True)
            acc = acc * alpha + _dot(p.astype(BF16), v)
            new += [m_new, l, acc]
        return tuple(new)

    init = (jnp.full((tq, 1), NEG, F32), jnp.zeros((tq, 1), F32), jnp.zeros((tq, dv), F32)) * 2
    m0, l0, a0, m1, l1, a1 = lax.fori_loop(0, qi + 1, body, init)
    o = a0 / l0 - lam * (a1 / l1)
    ms = jnp.mean(o * o, axis=-1, keepdims=True)
    o_ref[...] = (o * lax.rsqrt(ms + EPS) * g_ref[...] * out_scale).astype(o_ref.dtype)


def _df_prompt(lam, slopes, dq, dk, dv, subln_g, batch, seq, heads, out_scale):
    tq = min(seq, 256)
    nq = seq // tq
    w = 2 * DH
    smem = pl.BlockSpec(memory_space=pltpu.SMEM)
    return pl.pallas_call(
        functools.partial(_df_prompt_kernel, tq=tq, out_scale=out_scale),
        grid=(batch, heads, nq),
        in_specs=[smem, smem,
                  pl.BlockSpec((tq, w), lambda b, h, i: (b * nq + i, h)),
                  pl.BlockSpec((seq, w), lambda b, h, i: (b, h)),
                  pl.BlockSpec((seq, w), lambda b, h, i: (b, h)),
                  pl.BlockSpec((1, w), lambda b, h, i: (0, 0))],
        out_specs=pl.BlockSpec((tq, w), lambda b, h, i: (b * nq + i, h)),
        out_shape=jax.ShapeDtypeStruct(dq.shape, BF16),
        compiler_params=_params(("parallel", "parallel", "arbitrary")),
        name="df_prompt",
    )(lam, slopes, dq, dk, dv, subln_g.reshape(1, w))


QROWS = 8
PAGES_PER_STEP = 4


def _sample_kernel(pt_ref, lam_ref, slope_ref, sq_ref, dq_ref, g_ref, *refs, n_pages, pps, out_scale):
    del pt_ref
    new_refs = refs[0:4]
    page_refs = [refs[4 + 4 * n:8 + 4 * n] for n in range(pps)]
    osb_ref, odf_ref, qsb_s, qdf_s, c_s, asb_s, m_s, l_s, adf_s = refs[4 + 4 * pps:]
    p = pl.program_id(1)
    groups = 8
    rows = groups * QROWS
    past = n_pages * PAGE
    scale = DH ** -0.5
    rid = lax.broadcasted_iota(jnp.int32, (rows, PAGE), 0)
    lane = lax.broadcasted_iota(jnp.int32, (rows, PAGE), 1)
    tok = rid % QROWS
    r8 = lax.broadcasted_iota(jnp.int32, (PAGE, PAGE), 0)
    c8 = lax.broadcasted_iota(jnp.int32, (PAGE, PAGE), 1)
    later = jnp.where(r8 > c8, 1.0, 0.0).astype(BF16)

    def slope_rows():
        hid = lax.broadcasted_iota(jnp.int32, (rows, 1), 0) // (2 * QROWS)
        s = jnp.zeros((rows, 1), F32)
        for hh in range(groups // 2):
            s = jnp.where(hid == hh, slope_ref[hh], s)
        return s

    def head(ref, j):
        return ref[pl.ds(j, PAGE, stride=groups), :].astype(BF16)

    def grp(j):
        return slice(j * QROWS, (j + 1) * QROWS)

    def both(hh):
        return slice(2 * hh * QROWS, (2 * hh + 2) * QROWS)

    def process(blocks, kpos, sb_mask, df_mask):
        parts = []
        for k_ref, _, _, _ in blocks:
            z = jnp.concatenate([_dot_nt(qsb_s[grp(j), :], head(k_ref, j)) for j in range(groups)],
                                axis=0) * scale
            sp = _softplus(z)
            lr = -sp if sb_mask is None else jnp.where(sb_mask, -sp, 0.0)
            parts.append((z - sp + _dot_split(lr, later), jnp.sum(lr, axis=-1, keepdims=True)))
        c = c_s[...]
        acc = [asb_s[grp(j), :] for j in range(groups)]
        for (logw, lrsum), (_, v_ref, _, _) in zip(parts, blocks):
            w = jnp.exp(logw + c)
            if sb_mask is not None:
                w = jnp.where(sb_mask, w, 0.0)
            wb = w.astype(BF16)
            for j in range(groups):
                acc[j] = acc[j] + _dot(wb[grp(j), :], head(v_ref, j))
            c = c + lrsum
        c_s[...] = c
        for j in range(groups):
            asb_s[grp(j), :] = acc[j]
        slopes = slope_rows()
        scores = []
        for (_, _, dk_ref, _), kp in zip(blocks, kpos):
            s = jnp.concatenate([_dot_nt(qdf_s[grp(j), :], head(dk_ref, j)) for j in range(groups)],
                                axis=0) * scale - slopes * (past + tok - kp).astype(F32)
            scores.append(s if df_mask is None else jnp.where(df_mask, s, NEG))
        m_old = m_s[...]
        m_new = m_old
        for s in scores:
            m_new = jnp.maximum(m_new, jnp.max(s, axis=-1, keepdims=True))
        alpha = jnp.exp(m_old - m_new)
        l = l_s[...] * alpha
        accd = [adf_s[both(hh), :] * alpha[both(hh), :] for hh in range(groups // 2)]
        for s, (_, _, _, dv_ref) in zip(scores, blocks):
            pb = jnp.exp(s - m_new)
            l = l + jnp.sum(pb, axis=-1, keepdims=True)
            pb = pb.astype(BF16)
            for hh in range(groups // 2):
                vh = jnp.concatenate([head(dv_ref, hh), head(dv_ref, groups // 2 + hh)], axis=1)
                accd[hh] = accd[hh] + _dot(pb[both(hh), :], vh)
        m_s[...] = m_new
        l_s[...] = l
        for hh in range(groups // 2):
            adf_s[both(hh), :] = accd[hh]

    @pl.when(p == 0)
    def _():
        for j in range(groups):
            qsb_s[grp(j), :] = sq_ref[0, :, j * DH:(j + 1) * DH].astype(BF16)
            qdf_s[grp(j), :] = dq_ref[0, :, j * DH:(j + 1) * DH].astype(BF16)
        c_s[...] = jnp.zeros_like(c_s)
        asb_s[...] = jnp.zeros_like(asb_s)
        m_s[...] = jnp.full_like(m_s, NEG)
        l_s[...] = jnp.zeros_like(l_s)
        adf_s[...] = jnp.zeros_like(adf_s)
        process([new_refs], [past + lane], lane < tok, lane <= tok)

    @pl.when(p > 0)
    def _():
        first = n_pages - 1 - (p - 1) * pps
        process(page_refs, [(first - n) * PAGE + lane for n in range(pps)], None, None)

    @pl.when(p == n_pages // pps)
    def _():
        lam = lam_ref[0]
        osb_ref[0] = jnp.concatenate([asb_s[grp(j), :] for j in range(groups)], axis=1)
        an = adf_s[...] / l_s[...]
        parts = []
        for hh in range(groups // 2):
            o = an[grp(2 * hh), :] - lam * an[grp(2 * hh + 1), :]
            ms = jnp.mean(o * o, axis=-1, keepdims=True)
            parts.append(o * lax.rsqrt(ms + EPS) * g_ref[...] * out_scale)
        odf_ref[0] = jnp.concatenate(parts, axis=1)


def _sample_attention(page_table, lam, slopes, sq, dq, subln_g, new_kv, pools, page_base, *, out_scale):
    b, n_pages = page_table.shape
    d = sq.shape[-1]
    rows = 8 * QROWS
    blk = PAGE * 8
    pps = next(c for c in (PAGES_PER_STEP, 2, 1) if n_pages % c == 0)
    smem = pl.BlockSpec(memory_space=pltpu.SMEM)
    qspec = pl.BlockSpec((1, QROWS, d), lambda i, p, pt: (i, 0, 0))
    nspec = pl.BlockSpec((blk, DH), lambda i, p, pt: (i, 0))

    def pspec(n):
        return pl.BlockSpec((blk, DH), lambda i, p, pt: (
            page_base + pt[i, n_pages - 1 - (jnp.maximum(p, 1) - 1) * pps - n], 0))

    grid_spec = pltpu.PrefetchScalarGridSpec(
        num_scalar_prefetch=1,
        grid=(b, n_pages // pps + 1),
        in_specs=[smem, smem, qspec, qspec, pl.BlockSpec((1, 2 * DH), lambda i, p, pt: (0, 0)),
                  nspec, nspec, nspec, nspec] + [pspec(n) for n in range(pps) for _ in range(4)],
        out_specs=[qspec, qspec],
        scratch_shapes=[pltpu.VMEM((rows, DH), BF16), pltpu.VMEM((rows, DH), BF16),
                        pltpu.VMEM((rows, 1), F32), pltpu.VMEM((rows, DH), F32),
                        pltpu.VMEM((rows, 1), F32), pltpu.VMEM((rows, 1), F32),
                        pltpu.VMEM((rows, 2 * DH), F32)],
    )
    return pl.pallas_call(
        functools.partial(_sample_kernel, n_pages=n_pages, pps=pps, out_scale=out_scale),
        grid_spec=grid_spec,
        out_shape=[jax.ShapeDtypeStruct((b, QROWS, d), F32)] * 2,
        compiler_params=_params(("parallel", "arbitrary")),
        name="sample_attention",
    )(page_table, lam, slopes, sq, dq, subln_g.reshape(1, 2 * DH), *new_kv, *(pools * pps))


def _half_major(v, h):
    k = v.shape[0]
    return v.reshape(k, h, 2, DH).transpose(0, 2, 1, 3).reshape(k * 2 * h, DH)


def _merge_kernel(osb_ref, odf_ref, wsb_ref, wdf_ref, ga_ref, gb_ref, o_ref):
    ysb = _dot(osb_ref[...], wsb_ref[...])
    ydf = _dot(odf_ref[...], wdf_ref[...])
    o_ref[...] = (ga_ref[...].astype(F32) * ysb + gb_ref[...].astype(F32) * ydf).astype(o_ref.dtype)


def _merge(osb, odf, w_sb, w_df, gates):
    m, k = osb.shape
    n = w_sb.shape[1]
    tm = min(m, 512)
    tn = min(n, 1024)
    nj = n // tn
    return pl.pallas_call(
        _merge_kernel,
        grid=(m // tm, nj),
        in_specs=[pl.BlockSpec((tm, k), lambda i, j: (i, 0)),
                  pl.BlockSpec((tm, k), lambda i, j: (i, 0)),
                  pl.BlockSpec((k, tn), lambda i, j: (0, j)),
                  pl.BlockSpec((k, tn), lambda i, j: (0, j)),
                  pl.BlockSpec((tm, tn), lambda i, j: (i, j)),
                  pl.BlockSpec((tm, tn), lambda i, j: (i, j + nj))],
        out_specs=pl.BlockSpec((tm, tn), lambda i, j: (i, j)),
        out_shape=jax.ShapeDtypeStruct((m, n), BF16),
        compiler_params=_params(("parallel", "arbitrary")),
        name="mixer_merge",
    )(osb, odf, w_sb, w_df, gates, gates)


def _resid_kernel(a_ref, w_ref, x_ref, o_ref):
    o_ref[...] = x_ref[...] + _dot(a_ref[...], w_ref[...])


def _resid_matmul(a, w, x):
    m, k = a.shape
    n = w.shape[1]
    tm = min(m, 512)
    tn = min(n, 1024)
    return pl.pallas_call(
        _resid_kernel,
        grid=(m // tm, n // tn),
        in_specs=[pl.BlockSpec((tm, k), lambda i, j: (i, 0)),
                  pl.BlockSpec((k, tn), lambda i, j: (0, j)),
                  pl.BlockSpec((tm, tn), lambda i, j: (i, j))],
        out_specs=pl.BlockSpec((tm, tn), lambda i, j: (i, j)),
        out_shape=jax.ShapeDtypeStruct((m, n), F32),
        compiler_params=_params(("parallel", "arbitrary")),
        name="mixer_out",
    )(a, w, x)


def _extract_top(vals, key, payload, big):
    m = jnp.max(vals, axis=0, keepdims=True)
    kmin = jnp.min(jnp.where(vals == m, key, big), axis=0, keepdims=True)
    sel = key == kmin
    pay = kmin if payload is None else jnp.max(jnp.where(sel, payload, -1), axis=0, keepdims=True)
    return m, pay, jnp.where(sel, -jnp.inf, vals)


def _peer_route_kernel(y_ref, g_ref, wq_ref, keys_ref, h_ref, idx_ref, gate_ref,
                       s_s, v_s, i_s, *, heads):
    tt = y_ref.shape[0]
    halves = tt // LANES
    y = y_ref[...]
    ms = jnp.mean(y * y, axis=-1, keepdims=True)
    h = (y * lax.rsqrt(ms + EPS) * g_ref[...]).astype(BF16)
    h_ref[...] = h
    q = _dot(h, wq_ref[...]).astype(BF16)
    n_pc = 2 * heads
    for pc in range(n_pc):
        s = _dot_nt(keys_ref[pc], q[:, pc * DH:(pc + 1) * DH])
        for hf in range(halves):
            s_s[hf * n_pc + pc] = s[:, hf * LANES:(hf + 1) * LANES]

    key_iota = lax.broadcasted_iota(jnp.int32, (N_KEYS, LANES), 0)

    def stage1(jj, _):
        js = (2 * jj, 2 * jj + 1)
        vals = [s_s[j] for j in js]
        for a in range(TOPK):
            for c, j in enumerate(js):
                m, pay, vals[c] = _extract_top(vals[c], key_iota, None, N_KEYS)
                v_s[j, a:a + 1, :] = m
                i_s[j, a:a + 1, :] = pay
        return 0

    lax.fori_loop(0, halves * n_pc // 2, stage1, 0)

    sub = lax.broadcasted_iota(jnp.int32, (8, LANES), 0)
    pos = jnp.concatenate([sub, sub + 8] + [sub + a * TOPK for a in range(1, 8)]
                          + [(sub + 8) * TOPK], axis=0)

    def pairs(x1, x2):
        return jnp.concatenate(
            [x1[0:1] + x2[0:8], x1[0:1] + x2[8:16]]
            + [x1[a:a + 1] + x2[0:8] for a in range(1, 8)] + [x1[8:16] + x2[0:1]], axis=0)

    def stage2(jj, _):
        hf = jj // (heads // 2)
        ps = [2 * (jj % (heads // 2)) + c for c in range(2)]
        j1s = [hf * n_pc + 2 * p for p in ps]
        cand = [pairs(v_s[j1], v_s[j1 + 1]) for j1 in j1s]
        eidx = [pairs(i_s[j1] * N_KEYS, i_s[j1 + 1]) for j1 in j1s]
        tops, ids = ([], []), ([], [])
        for _k in range(TOPK):
            for c in range(2):
                m, pay, cand[c] = _extract_top(cand[c], pos, eidx[c], TOPK * TOPK)
                tops[c].append(m)
                ids[c].append(pay)
        for c in range(2):
            ex = jnp.exp(jnp.concatenate(tops[c], axis=0) - tops[c][0])
            gate = ex / jnp.sum(ex, axis=0, keepdims=True)
            r0 = pl.multiple_of(ps[c] * TOPK, TOPK)
            for hh in range(halves):
                @pl.when(hf == hh)
                def _(c=c, gate=gate, r0=r0, hh=hh):
                    idx_ref[pl.ds(r0, TOPK), hh * LANES:(hh + 1) * LANES] = jnp.concatenate(ids[c], axis=0)
                    gate_ref[pl.ds(r0, TOPK), hh * LANES:(hh + 1) * LANES] = gate
        return 0

    lax.fori_loop(0, halves * heads // 2, stage2, 0)


def _peer_route(y, norm_g, w_q, keys):
    n, d = y.shape
    n_pc = keys.shape[0]
    heads = n_pc // 2
    tt = min(n, 256)
    halves = tt // LANES
    rows = heads * TOPK
    return pl.pallas_call(
        functools.partial(_peer_route_kernel, heads=heads),
        grid=(n // tt,),
        in_specs=[pl.BlockSpec((tt, d), lambda i: (i, 0)),
                  pl.BlockSpec((1, d), lambda i: (0, 0)),
                  pl.BlockSpec(w_q.shape, lambda i: (0, 0)),
                  pl.BlockSpec(keys.shape, lambda i: (0, 0, 0))],
        out_specs=[pl.BlockSpec((tt, d), lambda i: (i, 0)),
                   pl.BlockSpec((rows, tt), lambda i: (0, i)),
                   pl.BlockSpec((rows, tt), lambda i: (0, i))],
        out_shape=[jax.ShapeDtypeStruct((n, d), BF16),
                   jax.ShapeDtypeStruct((rows, n), jnp.int32),
                   jax.ShapeDtypeStruct((rows, n), F32)],
        scratch_shapes=[pltpu.VMEM((halves * n_pc, N_KEYS, LANES), F32),
                        pltpu.VMEM((halves * n_pc, TOPK, LANES), F32),
                        pltpu.VMEM((halves * n_pc, TOPK, LANES), jnp.int32)],
        compiler_params=_params(("parallel",)),
        name="peer_route",
    )(y, norm_g.reshape(1, d), w_q, keys)


def _peer_expert_kernel(idx_head_ref, idx_ahead_ref, h_ref, gate_ref, y_ref, diag_ref, seg_ref,
                        segt_ref, uv_ref, o_ref, buf, sem, *, tb, picks):
    i = pl.program_id(0)
    nb = pl.num_programs(0)
    per_blk = tb * picks

    def start(e, slot, r):
        pltpu.make_async_copy(uv_ref.at[e], buf.at[slot, r], sem.at[slot]).start()

    def wait(slot):
        pltpu.make_async_copy(buf.at[slot], buf.at[slot], sem.at[slot]).wait()

    @pl.when(i == 0)
    def _():
        for b in range(AHEAD):
            def body(c, _):
                for u in range(ISSUE_UNROLL):
                    r = c * ISSUE_UNROLL + u
                    start(idx_head_ref[b * per_blk + r], b, r)
                return 0
            lax.fori_loop(0, per_blk // ISSUE_UNROLL, body, 0)

    slot = i % NSLOT
    ahead_slot = (i + AHEAD) % NSLOT
    wait(slot)

    def request(t, part):
        half = picks // 2
        for k in range(part * half, (part + 1) * half):
            start(idx_ahead_ref[t * picks + k], ahead_slot, t * picks + k)

    diag = diag_ref[...]
    rm = []
    for t in range(tb):
        request(t, 0)
        ub = buf[slot, pl.ds(t * picks, picks), 0:SLAB, :].reshape(picks * SLAB, LANES)
        rm.append(_dot_nt(h_ref[t], ub) * diag)
    part = _dot_split(jnp.concatenate(rm, axis=0), seg_ref[...])
    act = jnp.sum(part.reshape(tb, SLAB, picks), axis=1)
    gelu = 0.5 * act * (1.0 + lax.erf(act * math.sqrt(0.5)))
    coef = gate_ref[...] * gelu
    cx = _dot_split(coef, segt_ref[...])
    for t in range(tb):
        request(t, 1)
        vb = buf[slot, pl.ds(t * picks, picks), SLAB:2 * SLAB, :].reshape(picks * SLAB, LANES)
        e = (diag * cx[t:t + 1, :]).astype(BF16)
        o_ref[t] = y_ref[t] + _dot(e, vb)

    @pl.when(i == nb - 1)
    def _():
        for b in range(1, AHEAD + 1):
            wait((i + b) % NSLOT)


def _peer_experts(idx, gate, h_slab, y_slab, uv):
    n, picks = idx.shape
    tb = 8
    nb = n // tb
    assert nb >= NSLOT
    per_blk = tb * picks
    idx = idx.reshape(n * picks)
    j = np.arange(picks * SLAB)
    diag = jnp.asarray((j[None, :] % SLAB == np.arange(SLAB)[:, None]).astype(np.float32))
    seg_np = (j[:, None] // SLAB == np.arange(picks)[None, :]).astype(np.float32)
    seg = jnp.asarray(seg_np, dtype=BF16)
    segt = jnp.asarray(seg_np.T, dtype=BF16)
    const2 = lambda i: (0, 0)
    tok3 = lambda i: (i, 0, 0)
    return pl.pallas_call(
        functools.partial(_peer_expert_kernel, tb=tb, picks=picks),
        grid=(nb,),
        in_specs=[pl.BlockSpec((AHEAD * per_blk,), lambda i: (0,), memory_space=pltpu.SMEM),
                  pl.BlockSpec((per_blk,), lambda i: (jnp.minimum(i + AHEAD, nb - 1),),
                               memory_space=pltpu.SMEM),
                  pl.BlockSpec((tb, SLAB, LANES), tok3),
                  pl.BlockSpec((tb, picks), lambda i: (i, 0)),
                  pl.BlockSpec((tb, SLAB, LANES), tok3),
                  pl.BlockSpec(diag.shape, const2),
                  pl.BlockSpec(seg.shape, const2),
                  pl.BlockSpec(segt.shape, const2),
                  pl.BlockSpec(memory_space=pl.ANY)],
        out_specs=pl.BlockSpec((tb, SLAB, LANES), tok3),
        out_shape=jax.ShapeDtypeStruct(y_slab.shape, F32),
        scratch_shapes=[pltpu.VMEM((NSLOT, per_blk, 2 * SLAB, LANES), BF16),
                        pltpu.SemaphoreType.DMA((NSLOT,))],
        compiler_params=_params(("arbitrary",)),
        name="peer_experts",
    )(idx, idx, h_slab, gate, y_slab, diag, seg, segt, uv)


def _channel_mixer(y, norm_g, w_q, keys, uv):
    n, d = y.shape
    h, idx_t, gate_t = _peer_route(y, norm_g, w_q, keys)
    out = _peer_experts(idx_t.T, gate_t.T, h.reshape(n, SLAB, LANES), y.reshape(n, SLAB, LANES), uv)
    return out.reshape(n, d)


def kernel(x_prompt, x_sample, cache_sb_k, cache_sb_v, cache_df_k, cache_df_v, page_table, norm_mix_g, w_in, df_q_norm_g, df_k_norm_g, lambda_q1, lambda_k1, lambda_q2, lambda_k2, df_subln_g, w_branch_sb, w_branch_df, w_out, norm_ffn_g, peer_w_q, peer_sub_keys, peer_u, peer_v):
    batch, seq, d_model = x_prompt.shape
    dec_batch, dec_seq, _ = x_sample.shape
    depth = w_in.shape[0]
    h_sb = cache_sb_k.shape[3]
    h_df = cache_df_k.shape[3]
    d_sb = h_sb * DH
    d_df = h_df * 2 * DH
    n_pool = cache_sb_k.shape[1]
    heads = peer_sub_keys.shape[1]
    assert d_model == SLAB * LANES and cache_sb_k.shape[2] == PAGE and peer_sub_keys.shape[3] == N_KEYS
    assert h_sb == 8 and 2 * h_df == 8 and dec_seq <= QROWS
    slopes = jnp.asarray(2.0 ** (-8.0 * np.arange(1, h_df + 1) / h_df), F32)
    pools = (cache_sb_k.reshape(-1, DH), cache_sb_v.reshape(-1, DH), cache_df_k.reshape(-1, DH),
             _half_major(cache_df_v.reshape(-1, d_df), h_df))

    yp = x_prompt.reshape(batch * seq, d_model)
    ys = x_sample.reshape(dec_batch * dec_seq, d_model)
    outs = [[] for _ in range(8)]
    for l in range(depth):
        lambda_init = 0.8 - 0.6 * math.exp(-0.3 * l)
        out_scale = 1.0 - lambda_init
        lam = _lambda(lambda_q1[l], lambda_k1[l], lambda_q2[l], lambda_k2[l], lambda_init)
        wl = w_in[l].astype(BF16)
        cuts = np.cumsum([0, d_sb, d_sb, d_sb, d_df, d_df, d_df, 2 * d_model])
        w_sq, w_sk, w_sv, w_dq, w_dk, w_dv, w_g = [wl[:, a:b] for a, b in zip(cuts[:-1], cuts[1:])]
        w_sb = w_branch_sb[l].astype(BF16)
        w_df = w_branch_df[l].astype(BF16)
        w_o = w_out[l].astype(BF16)
        w_q = peer_w_q[l].astype(BF16)
        keys = peer_sub_keys[l].reshape(2 * heads, N_KEYS, DH).astype(BF16)
        n_exp = peer_u.shape[1]
        uv = jnp.concatenate([peer_u[l].astype(BF16).reshape(n_exp, SLAB, LANES),
                              peer_v[l].astype(BF16).reshape(n_exp, SLAB, LANES)], axis=1)

        def mixer_inputs(x, tag):
            h = _rmsnorm(x, norm_mix_g[l])
            (sq,) = _proj(h, w_sq, dtypes=(BF16,), name=f"proj_sq_{tag}")
            sk32, sk = _proj(h, w_sk, dtypes=(F32, BF16), name=f"proj_sk_{tag}")
            sv32, sv = _proj(h, w_sv, dtypes=(F32, BF16), name=f"proj_sv_{tag}")
            (dq,) = _proj(h, w_dq, dtypes=(BF16,), gain=df_q_norm_g[l], name=f"proj_dq_{tag}")
            dk32, dk = _proj(h, w_dk, dtypes=(F32, BF16), gain=df_k_norm_g[l], name=f"proj_dk_{tag}")
            dv32, dv = _proj(h, w_dv, dtypes=(F32, BF16), name=f"proj_dv_{tag}")
            (gates,) = _proj(h, w_g, dtypes=(BF16,), sigmoid=True, name=f"proj_gates_{tag}")
            return (sq, sk, sv, dq, dk, dv, gates), (sk32, sv32, dk32, dv32)

        def mixer_out(x, osb, odf, gates):
            merged = _merge(osb, odf, w_sb, w_df, gates)
            return _resid_matmul(merged, w_o, x)

        (sq, sk, sv, dq, dk, dv, gates), kv32 = mixer_inputs(yp, "p")
        osb = _sb_prompt(sq, sk, sv, batch, seq, h_sb)
        odf = _df_prompt(lam, slopes, dq, dk, dv, df_subln_g[l], batch, seq, h_df, out_scale)
        yp = mixer_out(yp, osb, odf, gates)
        for o, a in zip(outs[:4], kv32):
            o.append(a)

        hs = _rmsnorm(ys, norm_mix_g[l])
        (sq32,) = _proj(hs, w_sq, dtypes=(F32,), name="proj_sq_s")
        (sk32,) = _proj(hs, w_sk, dtypes=(F32,), name="proj_sk_s")
        (sv32,) = _proj(hs, w_sv, dtypes=(F32,), name="proj_sv_s")
        (dq32,) = _proj(hs, w_dq, dtypes=(F32,), gain=df_q_norm_g[l], name="proj_dq_s")
        (dk32,) = _proj(hs, w_dk, dtypes=(F32,), gain=df_k_norm_g[l], name="proj_dk_s")
        (dv32,) = _proj(hs, w_dv, dtypes=(F32,), name="proj_dv_s")
        (gates_s,) = _proj(hs, w_g, dtypes=(BF16,), sigmoid=True, name="proj_gates_s")

        def query_rows(a):
            a = a.reshape(dec_batch, dec_seq, a.shape[-1])
            return jnp.pad(a, ((0, 0), (0, QROWS - dec_seq), (0, 0)))

        def key_block(a):
            a = a.reshape(dec_batch, dec_seq, 8 * DH)
            return jnp.pad(a, ((0, 0), (0, PAGE - dec_seq), (0, 0))).reshape(dec_batch * PAGE * 8, DH)

        def token_rows(a):
            return a[:, :dec_seq].reshape(dec_batch * dec_seq, a.shape[-1]).astype(BF16)

        new_kv = (key_block(sk32), key_block(sv32), key_block(dk32),
                  key_block(_half_major(dv32, h_df).reshape(dec_batch * dec_seq, d_df)))
        osb_s, odf_s = _sample_attention(
            page_table, lam, slopes, query_rows(sq32), query_rows(dq32), df_subln_g[l], new_kv, pools,
            l * n_pool, out_scale=out_scale)
        ys = mixer_out(ys, token_rows(osb_s), token_rows(odf_s), gates_s)
        for o, a in zip(outs[4:], (sk32, sv32, dk32, dv32)):
            o.append(a)

        yp = _channel_mixer(yp, norm_ffn_g[l], w_q, keys, uv)
        ys = _channel_mixer(ys, norm_ffn_g[l], w_q, keys, uv)

    def stack(items, lead, tail):
        return jnp.stack(items).reshape((depth,) + lead + tail)

    pl_, sl_ = (batch, seq), (dec_batch, dec_seq)
    return (yp.reshape(batch, seq, d_model), ys.reshape(dec_batch, dec_seq, d_model),
            stack(outs[0], pl_, (h_sb, DH)), stack(outs[1], pl_, (h_sb, DH)),
            stack(outs[2], pl_, (h_df, 2, DH)), stack(outs[3], pl_, (h_df, 2 * DH)),
            stack(outs[4], sl_, (h_sb, DH)), stack(outs[5], sl_, (h_sb, DH)),
            stack(outs[6], sl_, (h_df, 2, DH)), stack(outs[7], sl_, (h_df, 2 * DH)))
```

```python
import functools
import math

import numpy as np
import jax
import jax.numpy as jnp
from jax import lax
from jax.experimental import pallas as pl
from jax.experimental.pallas import tpu as pltpu

F32 = jnp.float32
BF16 = jnp.bfloat16
EPS = 1e-6
LANES = 128
DH = 128
PAGE = 128
N_KEYS = 128
TOPK = 16
SLAB = 16
ISSUE_UNROLL = 8
NSLOT = 3
AHEAD = NSLOT - 1
VMEM_LIMIT = 48 * 1024 * 1024
NEG = -1e30

_NT = (((1,), (1,)), ((), ()))


def _dot(a, b):
    return jnp.dot(a, b, preferred_element_type=F32)


def _dot_nt(a, b):
    return lax.dot_general(a, b, _NT, preferred_element_type=F32)


def _split_bf16(x):
    hi = x.astype(BF16)
    lo = (x - hi.astype(F32)).astype(BF16)
    return hi, lo


def _dot_split(x, w):
    hi, lo = _split_bf16(x)
    return _dot(hi, w) + _dot(lo, w)


def _softplus(z):
    return jnp.maximum(z, 0.0) + jnp.log1p(jnp.exp(-jnp.abs(z)))


def _params(sem):
    return pltpu.CompilerParams(dimension_semantics=sem, vmem_limit_bytes=VMEM_LIMIT)


def _rms_kernel(x_ref, g_ref, o_ref):
    x = x_ref[...]
    ms = jnp.mean(x * x, axis=-1, keepdims=True)
    o_ref[...] = (x * lax.rsqrt(ms + EPS) * g_ref[...]).astype(o_ref.dtype)


def _rmsnorm(x, g):
    m, d = x.shape
    tm = min(m, 512)
    return pl.pallas_call(
        _rms_kernel,
        grid=(m // tm,),
        in_specs=[pl.BlockSpec((tm, d), lambda i: (i, 0)), pl.BlockSpec((1, d), lambda i: (0, 0))],
        out_specs=pl.BlockSpec((tm, d), lambda i: (i, 0)),
        out_shape=jax.ShapeDtypeStruct((m, d), BF16),
        compiler_params=_params(("parallel",)),
        name="rmsnorm",
    )(x, g.reshape(1, d))


def _proj_kernel(h_ref, w_ref, g_ref, *o_refs, norm, sigmoid, cache_rows):
    acc = _dot(h_ref[...], w_ref[...])
    if norm:
        g = g_ref[...]
        parts = []
        for c in range(acc.shape[1] // DH):
            blk = acc[:, c * DH:(c + 1) * DH]
            ms = jnp.mean(blk * blk, axis=-1, keepdims=True)
            parts.append(blk * lax.rsqrt(ms + EPS) * g)
        acc = jnp.concatenate(parts, axis=1)
    if sigmoid:
        acc = jax.nn.sigmoid(acc)
    if cache_rows is not None:
        cache_ref, o_refs = o_refs[0], o_refs[1:]
        tm = acc.shape[0]
        for c, r in enumerate(cache_rows):
            cache_ref[pl.ds(r, tm, stride=len(cache_rows)), :] = acc[:, c * DH:(c + 1) * DH]
    for o_ref in o_refs:
        o_ref[...] = acc.astype(o_ref.dtype)


def _proj(h, w, *, dtypes, gain=None, sigmoid=False, cache_rows=None, name):
    m, k = h.shape
    n = w.shape[1]
    tm = min(m, 512)
    tn = min(n, 1024)
    norm = gain is not None
    g = (gain if norm else jnp.ones((DH,), F32)).reshape(1, DH).astype(F32)
    out_specs = [pl.BlockSpec((tm, tn), lambda i, j: (i, j)) for _ in dtypes]
    out_shape = [jax.ShapeDtypeStruct((m, n), dt) for dt in dtypes]
    if cache_rows is not None:
        assert tn == n == len(cache_rows) * DH and dtypes[0] == F32
        out_specs[0] = pl.BlockSpec((tm * len(cache_rows), DH), lambda i, j: (i, 0))
        out_shape[0] = jax.ShapeDtypeStruct((m * len(cache_rows), DH), F32)
    outs = pl.pallas_call(
        functools.partial(_proj_kernel, norm=norm, sigmoid=sigmoid, cache_rows=cache_rows),
        grid=(m // tm, n // tn),
        in_specs=[pl.BlockSpec((tm, k), lambda i, j: (i, 0)),
                  pl.BlockSpec((k, tn), lambda i, j: (0, j)),
                  pl.BlockSpec((1, DH), lambda i, j: (0, 0))],
        out_specs=out_specs,
        out_shape=out_shape,
        compiler_params=_params(("parallel", "arbitrary")),
        name=name,
    )(h, w, g)
    return outs


def _lambda_kernel(q1_ref, k1_ref, q2_ref, k2_ref, o_ref, *, lambda_init):
    a = jnp.exp(jnp.sum(q1_ref[...] * k1_ref[...], axis=-1, keepdims=True))
    b = jnp.exp(jnp.sum(q2_ref[...] * k2_ref[...], axis=-1, keepdims=True))
    o_ref[...] = a - b + lambda_init


def _lambda(q1, k1, q2, k2, lambda_init):
    d = q1.shape[0]
    spec = pl.BlockSpec((1, d), lambda: (0, 0))
    out = pl.pallas_call(
        functools.partial(_lambda_kernel, lambda_init=lambda_init),
        in_specs=[spec] * 4,
        out_specs=pl.BlockSpec((1, 1), lambda: (0, 0)),
        out_shape=jax.ShapeDtypeStruct((1, 1), F32),
        name="lambda_scalar",
    )(q1.reshape(1, d), k1.reshape(1, d), q2.reshape(1, d), k2.reshape(1, d))
    return out.reshape(1)


def _sb_prompt_kernel(q_ref, k_ref, v_ref, o_ref, *, tq):
    qi = pl.program_id(2)
    q = q_ref[...]
    row = lax.broadcasted_iota(jnp.int32, (tq, tq), 0)
    col = lax.broadcasted_iota(jnp.int32, (tq, tq), 1)
    later = jnp.where(row > col, 1.0, 0.0).astype(BF16)
    scale = DH ** -0.5

    def body(it, carry):
        acc, c = carry
        kj = qi - it
        start = pl.multiple_of(kj * tq, tq)
        k = k_ref[pl.ds(start, tq), :]
        v = v_ref[pl.ds(start, tq), :]
        z = _dot_nt(q, k) * scale
        mask = (col + kj * tq) < (row + qi * tq)
        sp = _softplus(z)
        lr = jnp.where(mask, -sp, 0.0)
        after = _dot_split(lr, later) + c
        w = jnp.where(mask, jnp.exp(z - sp + after), 0.0)
        acc = acc + _dot(w.astype(BF16), v)
        c = c + jnp.sum(lr, axis=-1, keepdims=True)
        return acc, c

    acc, _ = lax.fori_loop(0, qi + 1, body, (jnp.zeros((tq, DH), F32), jnp.zeros((tq, 1), F32)))
    o_ref[...] = acc.astype(o_ref.dtype)


def _sb_prompt(sq, sk, sv, batch, seq, heads):
    tq = min(seq, 256)
    nq = seq // tq
    return pl.pallas_call(
        functools.partial(_sb_prompt_kernel, tq=tq),
        grid=(batch, heads, nq),
        in_specs=[pl.BlockSpec((tq, DH), lambda b, h, i: (b * nq + i, h)),
                  pl.BlockSpec((seq, DH), lambda b, h, i: (b, h)),
                  pl.BlockSpec((seq, DH), lambda b, h, i: (b, h))],
        out_specs=pl.BlockSpec((tq, DH), lambda b, h, i: (b * nq + i, h)),
        out_shape=jax.ShapeDtypeStruct(sq.shape, BF16),
        compiler_params=_params(("parallel", "parallel", "arbitrary")),
        name="sb_prompt",
    )(sq, sk, sv)


def _df_prompt_kernel(lam_ref, slope_ref, q_ref, k_ref, v_ref, g_ref, o_ref, *, tq, out_scale):
    h = pl.program_id(1)
    qi = pl.program_id(2)
    q = q_ref[...]
    row = lax.broadcasted_iota(jnp.int32, (tq, tq), 0)
    col = lax.broadcasted_iota(jnp.int32, (tq, tq), 1)
    slope = slope_ref[h]
    lam = lam_ref[0]
    scale = DH ** -0.5
    dv = 2 * DH

    def body(kj, carry):
        start = pl.multiple_of(kj * tq, tq)
        k = k_ref[pl.ds(start, tq), :]
        v = v_ref[pl.ds(start, tq), :]
        dist = (row + qi * tq) - (col + kj * tq)
        bias = slope * dist.astype(F32)
        mask = dist >= 0
        new = []
        for c in range(2):
            m, l, acc = carry[3 * c:3 * c + 3]
            s = _dot_nt(q[:, c * DH:(c + 1) * DH], k[:, c * DH:(c + 1) * DH]) * scale - bias
            s = jnp.where(mask, s, NEG)
            m_new = jnp.maximum(m, jnp.max(s, axis=-1, keepdims=True))
            alpha = jnp.exp(m - m_new)
            p = jnp.exp(s - m_new)
            l = l * alpha + jnp.sum(p, axis=-1, keepdims=True)
            acc = acc * alpha + _dot(p.astype(BF16), v)
            new += [m_new, l, acc]
        return tuple(new)

    init = (jnp.full((tq, 1), NEG, F32), jnp.zeros((tq, 1), F32), jnp.zeros((tq, dv), F32)) * 2
    m0, l0, a0, m1, l1, a1 = lax.fori_loop(0, qi + 1, body, init)
    o = a0 / l0 - lam * (a1 / l1)
    ms = jnp.mean(o * o, axis=-1, keepdims=True)
    o_ref[...] = (o * lax.rsqrt(ms + EPS) * g_ref[...] * out_scale).astype(o_ref.dtype)


def _df_prompt(lam, slopes, dq, dk, dv, subln_g, batch, seq, heads, out_scale):
    tq = min(seq, 256)
    nq = seq // tq
    w = 2 * DH
    smem = pl.BlockSpec(memory_space=pltpu.SMEM)
    return pl.pallas_call(
        functools.partial(_df_prompt_kernel, tq=tq, out_scale=out_scale),
        grid=(batch, heads, nq),
        in_specs=[smem, smem,
                  pl.BlockSpec((tq, w), lambda b, h, i: (b * nq + i, h)),
                  pl.BlockSpec((seq, w), lambda b, h, i: (b, h)),
                  pl.BlockSpec((seq, w), lambda b, h, i: (b, h)),
                  pl.BlockSpec((1, w), lambda b, h, i: (0, 0))],
        out_specs=pl.BlockSpec((tq, w), lambda b, h, i: (b * nq + i, h)),
        out_shape=jax.ShapeDtypeStruct(dq.shape, BF16),
        compiler_params=_params(("parallel", "parallel", "arbitrary")),
        name="df_prompt",
    )(lam, slopes, dq, dk, dv, subln_g.reshape(1, w))


QROWS = 8
PAGES_PER_STEP = 4


def _sample_kernel(pt_ref, lam_ref, slope_ref, sq_ref, dq_ref, g_ref, *refs, n_pages, pps, out_scale):
    del pt_ref
    new_refs = refs[0:4]
    page_refs = [refs[4 + 4 * n:8 + 4 * n] for n in range(pps)]
    osb_ref, odf_ref, qsb_s, qdf_s, c_s, asb_s, m_s, l_s, adf_s = refs[4 + 4 * pps:]
    p = pl.program_id(1)
    groups = 8
    rows = groups * QROWS
    past = n_pages * PAGE
    scale = DH ** -0.5
    rid = lax.broadcasted_iota(jnp.int32, (rows, PAGE), 0)
    lane = lax.broadcasted_iota(jnp.int32, (rows, PAGE), 1)
    tok = rid % QROWS
    r8 = lax.broadcasted_iota(jnp.int32, (PAGE, PAGE), 0)
    c8 = lax.broadcasted_iota(jnp.int32, (PAGE, PAGE), 1)
    later = jnp.where(r8 > c8, 1.0, 0.0).astype(BF16)

    def slope_rows():
        hid = lax.broadcasted_iota(jnp.int32, (rows, 1), 0) // (2 * QROWS)
        s = jnp.zeros((rows, 1), F32)
        for hh in range(groups // 2):
            s = jnp.where(hid == hh, slope_ref[hh], s)
        return s

    def head(ref, j):
        return ref[pl.ds(j, PAGE, stride=groups), :].astype(BF16)

    def grp(j):
        return slice(j * QROWS, (j + 1) * QROWS)

    def both(hh):
        return slice(2 * hh * QROWS, (2 * hh + 2) * QROWS)

    def process(blocks, kpos, sb_mask, df_mask):
        parts = []
        for k_ref, _, _, _ in blocks:
            z = jnp.concatenate([_dot_nt(qsb_s[grp(j), :], head(k_ref, j)) for j in range(groups)],
                                axis=0) * scale
            sp = _softplus(z)
            lr = -sp if sb_mask is None else jnp.where(sb_mask, -sp, 0.0)
            parts.append((z - sp + _dot_split(lr, later), jnp.sum(lr, axis=-1, keepdims=True)))
        c = c_s[...]
        acc = [asb_s[grp(j), :] for j in range(groups)]
        for (logw, lrsum), (_, v_ref, _, _) in zip(parts, blocks):
            w = jnp.exp(logw + c)
            if sb_mask is not None:
                w = jnp.where(sb_mask, w, 0.0)
            wb = w.astype(BF16)
            for j in range(groups):
                acc[j] = acc[j] + _dot(wb[grp(j), :], head(v_ref, j))
            c = c + lrsum
        c_s[...] = c
        for j in range(groups):
            asb_s[grp(j), :] = acc[j]
        slopes = slope_rows()
        scores = []
        for (_, _, dk_ref, _), kp in zip(blocks, kpos):
            s = jnp.concatenate([_dot_nt(qdf_s[grp(j), :], head(dk_ref, j)) for j in range(groups)],
                                axis=0) * scale - slopes * (past + tok - kp).astype(F32)
            scores.append(s if df_mask is None else jnp.where(df_mask, s, NEG))
        m_old = m_s[...]
        m_new = m_old
        for s in scores:
            m_new = jnp.maximum(m_new, jnp.max(s, axis=-1, keepdims=True))
        alpha = jnp.exp(m_old - m_new)
        l = l_s[...] * alpha
        accd = [adf_s[both(hh), :] * alpha[both(hh), :] for hh in range(groups // 2)]
        for s, (_, _, _, dv_ref) in zip(scores, blocks):
            pb = jnp.exp(s - m_new)
            l = l + jnp.sum(pb, axis=-1, keepdims=True)
            pb = pb.astype(BF16)
            for hh in range(groups // 2):
                vh = jnp.concatenate([head(dv_ref, hh), head(dv_ref, groups // 2 + hh)], axis=1)
                accd[hh] = accd[hh] + _dot(pb[both(hh), :], vh)
        m_s[...] = m_new
        l_s[...] = l
        for hh in range(groups // 2):
            adf_s[both(hh), :] = accd[hh]

    @pl.when(p == 0)
    def _():
        for j in range(groups):
            qsb_s[grp(j), :] = sq_ref[0, :, j * DH:(j + 1) * DH].astype(BF16)
            qdf_s[grp(j), :] = dq_ref[0, :, j * DH:(j + 1) * DH].astype(BF16)
        c_s[...] = jnp.zeros_like(c_s)
        asb_s[...] = jnp.zeros_like(asb_s)
        m_s[...] = jnp.full_like(m_s, NEG)
        l_s[...] = jnp.zeros_like(l_s)
        adf_s[...] = jnp.zeros_like(adf_s)
        process([new_refs], [past + lane], lane < tok, lane <= tok)

    @pl.when(p > 0)
    def _():
        first = n_pages - 1 - (p - 1) * pps
        process(page_refs, [(first - n) * PAGE + lane for n in range(pps)], None, None)

    @pl.when(p == n_pages // pps)
    def _():
        lam = lam_ref[0]
        osb_ref[0] = jnp.concatenate([asb_s[grp(j), :] for j in range(groups)], axis=1)
        an = adf_s[...] / l_s[...]
        parts = []
        for hh in range(groups // 2):
            o = an[grp(2 * hh), :] - lam * an[grp(2 * hh + 1), :]
            ms = jnp.mean(o * o, axis=-1, keepdims=True)
            parts.append(o * lax.rsqrt(ms + EPS) * g_ref[...] * out_scale)
        odf_ref[0] = jnp.concatenate(parts, axis=1)


def _sample_attention(page_table, lam, slopes, sq, dq, subln_g, new_kv, pools, page_base, *, out_scale):
    b, n_pages = page_table.shape
    d = sq.shape[-1]
    rows = 8 * QROWS
    blk = PAGE * 8
    pps = next(c for c in (PAGES_PER_STEP, 2, 1) if n_pages % c == 0)
    smem = pl.BlockSpec(memory_space=pltpu.SMEM)
    qspec = pl.BlockSpec((1, QROWS, d), lambda i, p, pt: (i, 0, 0))
    nspec = pl.BlockSpec((blk, DH), lambda i, p, pt: (i, 0))

    def pspec(n):
        return pl.BlockSpec((blk, DH), lambda i, p, pt: (
            page_base + pt[i, n_pages - 1 - (jnp.maximum(p, 1) - 1) * pps - n], 0))

    grid_spec = pltpu.PrefetchScalarGridSpec(
        num_scalar_prefetch=1,
        grid=(b, n_pages // pps + 1),
        in_specs=[smem, smem, qspec, qspec, pl.BlockSpec((1, 2 * DH), lambda i, p, pt: (0, 0)),
                  nspec, nspec, nspec, nspec] + [pspec(n) for n in range(pps) for _ in range(4)],
        out_specs=[qspec, qspec],
        scratch_shapes=[pltpu.VMEM((rows, DH), BF16), pltpu.VMEM((rows, DH), BF16),
                        pltpu.VMEM((rows, 1), F32), pltpu.VMEM((rows, DH), F32),
                        pltpu.VMEM((rows, 1), F32), pltpu.VMEM((rows, 1), F32),
                        pltpu.VMEM((rows, 2 * DH), F32)],
    )
    return pl.pallas_call(
        functools.partial(_sample_kernel, n_pages=n_pages, pps=pps, out_scale=out_scale),
        grid_spec=grid_spec,
        out_shape=[jax.ShapeDtypeStruct((b, QROWS, d), F32)] * 2,
        compiler_params=_params(("parallel", "arbitrary")),
        name="sample_attention",
    )(page_table, lam, slopes, sq, dq, subln_g.reshape(1, 2 * DH), *new_kv, *(pools * pps))


def _half_major(v, h):
    k = v.shape[0]
    return v.reshape(k, h, 2, DH).transpose(0, 2, 1, 3).reshape(k * 2 * h, DH)


def _merge_kernel(osb_ref, odf_ref, wsb_ref, wdf_ref, ga_ref, gb_ref, o_ref):
    ysb = _dot(osb_ref[...], wsb_ref[...])
    ydf = _dot(odf_ref[...], wdf_ref[...])
    o_ref[...] = (ga_ref[...].astype(F32) * ysb + gb_ref[...].astype(F32) * ydf).astype(o_ref.dtype)


def _merge(osb, odf, w_sb, w_df, gates):
    m, k = osb.shape
    n = w_sb.shape[1]
    tm = min(m, 512)
    tn = min(n, 1024)
    nj = n // tn
    return pl.pallas_call(
        _merge_kernel,
        grid=(m // tm, nj),
        in_specs=[pl.BlockSpec((tm, k), lambda i, j: (i, 0)),
                  pl.BlockSpec((tm, k), lambda i, j: (i, 0)),
                  pl.BlockSpec((k, tn), lambda i, j: (0, j)),
                  pl.BlockSpec((k, tn), lambda i, j: (0, j)),
                  pl.BlockSpec((tm, tn), lambda i, j: (i, j)),
                  pl.BlockSpec((tm, tn), lambda i, j: (i, j + nj))],
        out_specs=pl.BlockSpec((tm, tn), lambda i, j: (i, j)),
        out_shape=jax.ShapeDtypeStruct((m, n), BF16),
        compiler_params=_params(("parallel", "arbitrary")),
        name="mixer_merge",
    )(osb, odf, w_sb, w_df, gates, gates)


def _resid_kernel(a_ref, w_ref, x_ref, o_ref):
    o_ref[...] = x_ref[...] + _dot(a_ref[...], w_ref[...])


def _resid_matmul(a, w, x):
    m, k = a.shape
    n = w.shape[1]
    tm = min(m, 512)
    tn = min(n, 1024)
    return pl.pallas_call(
        _resid_kernel,
        grid=(m // tm, n // tn),
        in_specs=[pl.BlockSpec((tm, k), lambda i, j: (i, 0)),
                  pl.BlockSpec((k, tn), lambda i, j: (0, j)),
                  pl.BlockSpec((tm, tn), lambda i, j: (i, j))],
        out_specs=pl.BlockSpec((tm, tn), lambda i, j: (i, j)),
        out_shape=jax.ShapeDtypeStruct((m, n), F32),
        compiler_params=_params(("parallel", "arbitrary")),
        name="mixer_out",
    )(a, w, x)


def _extract_top(vals, key, payload, big):
    m = jnp.max(vals, axis=0, keepdims=True)
    kmin = jnp.min(jnp.where(vals == m, key, big), axis=0, keepdims=True)
    sel = key == kmin
    pay = kmin if payload is None else jnp.max(jnp.where(sel, payload, -1), axis=0, keepdims=True)
    return m, pay, jnp.where(sel, -jnp.inf, vals)


def _peer_route_kernel(y_ref, g_ref, wq_ref, keys_ref, h_ref, idx_ref, gate_ref,
                       s_s, v_s, i_s, *, heads):
    tt = y_ref.shape[0]
    halves = tt // LANES
    y = y_ref[...]
    ms = jnp.mean(y * y, axis=-1, keepdims=True)
    h = (y * lax.rsqrt(ms + EPS) * g_ref[...]).astype(BF16)
    h_ref[...] = h
    q = _dot(h, wq_ref[...]).astype(BF16)
    n_pc = 2 * heads
    for pc in range(n_pc):
        s = _dot_nt(keys_ref[pc], q[:, pc * DH:(pc + 1) * DH])
        for hf in range(halves):
            s_s[hf * n_pc + pc] = s[:, hf * LANES:(hf + 1) * LANES]

    key_iota = lax.broadcasted_iota(jnp.int32, (N_KEYS, LANES), 0)

    def stage1(jj, _):
        js = (2 * jj, 2 * jj + 1)
        vals = [s_s[j] for j in js]
        for a in range(TOPK):
            for c, j in enumerate(js):
                m, pay, vals[c] = _extract_top(vals[c], key_iota, None, N_KEYS)
                v_s[j, a:a + 1, :] = m
                i_s[j, a:a + 1, :] = pay
        return 0

    lax.fori_loop(0, halves * n_pc // 2, stage1, 0)

    sub = lax.broadcasted_iota(jnp.int32, (8, LANES), 0)
    pos = jnp.concatenate([sub, sub + 8] + [sub + a * TOPK for a in range(1, 8)]
                          + [(sub + 8) * TOPK], axis=0)

    def pairs(x1, x2):
        return jnp.concatenate(
            [x1[0:1] + x2[0:8], x1[0:1] + x2[8:16]]
            + [x1[a:a + 1] + x2[0:8] for a in range(1, 8)] + [x1[8:16] + x2[0:1]], axis=0)

    def stage2(jj, _):
        hf = jj // (heads // 2)
        ps = [2 * (jj % (heads // 2)) + c for c in range(2)]
        j1s = [hf * n_pc + 2 * p for p in ps]
        cand = [pairs(v_s[j1], v_s[j1 + 1]) for j1 in j1s]
        eidx = [pairs(i_s[j1] * N_KEYS, i_s[j1 + 1]) for j1 in j1s]
        tops, ids = ([], []), ([], [])
        for _k in range(TOPK):
            for c in range(2):
                m, pay, cand[c] = _extract_top(cand[c], pos, eidx[c], TOPK * TOPK)
                tops[c].append(m)
                ids[c].append(pay)
        for c in range(2):
            ex = jnp.exp(jnp.concatenate(tops[c], axis=0) - tops[c][0])
            gate = ex / jnp.sum(ex, axis=0, keepdims=True)
            r0 = pl.multiple_of(ps[c] * TOPK, TOPK)
            for hh in range(halves):
                @pl.when(hf == hh)
                def _(c=c, gate=gate, r0=r0, hh=hh):
                    idx_ref[pl.ds(r0, TOPK), hh * LANES:(hh + 1) * LANES] = jnp.concatenate(ids[c], axis=0)
                    gate_ref[pl.ds(r0, TOPK), hh * LANES:(hh + 1) * LANES] = gate
        return 0

    lax.fori_loop(0, halves * heads // 2, stage2, 0)


def _peer_route(y, norm_g, w_q, keys):
    n, d = y.shape
    n_pc = keys.shape[0]
    heads = n_pc // 2
    tt = min(n, 256)
    halves = tt // LANES
    rows = heads * TOPK
    return pl.pallas_call(
        functools.partial(_peer_route_kernel, heads=heads),
        grid=(n // tt,),
        in_specs=[pl.BlockSpec((tt, d), lambda i: (i, 0)),
                  pl.BlockSpec((1, d), lambda i: (0, 0)),
                  pl.BlockSpec(w_q.shape, lambda i: (0, 0)),
                  pl.BlockSpec(keys.shape, lambda i: (0, 0, 0))],
        out_specs=[pl.BlockSpec((tt, d), lambda i: (i, 0)),
                   pl.BlockSpec((rows, tt), lambda i: (0, i)),
                   pl.BlockSpec((rows, tt), lambda i: (0, i))],
        out_shape=[jax.ShapeDtypeStruct((n, d), BF16),
                   jax.ShapeDtypeStruct((rows, n), jnp.int32),
                   jax.ShapeDtypeStruct((rows, n), F32)],
        scratch_shapes=[pltpu.VMEM((halves * n_pc, N_KEYS, LANES), F32),
                        pltpu.VMEM((halves * n_pc, TOPK, LANES), F32),
                        pltpu.VMEM((halves * n_pc, TOPK, LANES), jnp.int32)],
        compiler_params=_params(("parallel",)),
        name="peer_route",
    )(y, norm_g.reshape(1, d), w_q, keys)


def _peer_expert_kernel(idx_head_ref, idx_ahead_ref, h_ref, gate_ref, y_ref, diag_ref, seg_ref,
                        segt_ref, uv_ref, o_ref, buf, sem, *, tb, picks):
    i = pl.program_id(0)
    nb = pl.num_programs(0)
    per_blk = tb * picks

    def start(e, slot, r):
        pltpu.make_async_copy(uv_ref.at[e], buf.at[slot, r], sem.at[slot]).start()

    def wait(slot):
        pltpu.make_async_copy(buf.at[slot], buf.at[slot], sem.at[slot]).wait()

    @pl.when(i == 0)
    def _():
        for b in range(AHEAD):
            def body(c, _):
                for u in range(ISSUE_UNROLL):
                    r = c * ISSUE_UNROLL + u
                    start(idx_head_ref[b * per_blk + r], b, r)
                return 0
            lax.fori_loop(0, per_blk // ISSUE_UNROLL, body, 0)

    def step(slot):
        ahead_slot = (slot + AHEAD) % NSLOT
        wait(slot)
        pending = iter(range(per_blk))

        def request(count):
            for _ in range(count):
                r = next(pending)
                start(idx_ahead_ref[r], ahead_slot, r)

        first, mid = (5 * picks) // 16, picks // 2
        last = picks - first - (2 * mid) // tb
        diag = diag_ref[...]
        rm = []
        for t in range(tb):
            request(first)
            ub = buf[slot, pl.ds(t * picks, picks), 0:SLAB, :].reshape(picks * SLAB, LANES)
            rm.append(_dot_nt(h_ref[t], ub) * diag)
        request(mid)
        part = _dot_split(jnp.concatenate(rm, axis=0), seg_ref[...])
        act = jnp.sum(part.reshape(tb, SLAB, picks), axis=1)
        gelu = 0.5 * act * (1.0 + lax.erf(act * math.sqrt(0.5)))
        coef = gate_ref[...] * gelu
        request(mid)
        cx = _dot_split(coef, segt_ref[...])
        for t in range(tb):
            request(last)
            vb = buf[slot, pl.ds(t * picks, picks), SLAB:2 * SLAB, :].reshape(picks * SLAB, LANES)
            e = (diag * cx[t:t + 1, :]).astype(BF16)
            o_ref[t] = y_ref[t] + _dot(e, vb)
        assert next(pending, None) is None

        @pl.when(i == nb - 1)
        def _():
            for b in range(1, AHEAD + 1):
                wait((slot + b) % NSLOT)

    for s in range(NSLOT):
        pl.when(i % NSLOT == s)(functools.partial(step, s))


def _peer_experts(idx, gate, h_slab, y_slab, uv):
    n, picks = idx.shape
    tb = 8
    nb = n // tb
    assert nb >= NSLOT
    per_blk = tb * picks
    idx = idx.reshape(n * picks)
    j = np.arange(picks * SLAB)
    diag = jnp.asarray((j[None, :] % SLAB == np.arange(SLAB)[:, None]).astype(np.float32))
    seg_np = (j[:, None] // SLAB == np.arange(picks)[None, :]).astype(np.float32)
    seg = jnp.asarray(seg_np, dtype=BF16)
    segt = jnp.asarray(seg_np.T, dtype=BF16)
    const2 = lambda i: (0, 0)
    tok3 = lambda i: (i, 0, 0)
    return pl.pallas_call(
        functools.partial(_peer_expert_kernel, tb=tb, picks=picks),
        grid=(nb,),
        in_specs=[pl.BlockSpec((AHEAD * per_blk,), lambda i: (0,), memory_space=pltpu.SMEM),
                  pl.BlockSpec((per_blk,), lambda i: (jnp.minimum(i + AHEAD, nb - 1),),
                               memory_space=pltpu.SMEM),
                  pl.BlockSpec((tb, SLAB, LANES), tok3),
                  pl.BlockSpec((tb, picks), lambda i: (i, 0)),
                  pl.BlockSpec((tb, SLAB, LANES), tok3),
                  pl.BlockSpec(diag.shape, const2),
                  pl.BlockSpec(seg.shape, const2),
                  pl.BlockSpec(segt.shape, const2),
                  pl.BlockSpec(memory_space=pl.ANY)],
        out_specs=pl.BlockSpec((tb, SLAB, LANES), tok3),
        out_shape=jax.ShapeDtypeStruct(y_slab.shape, F32),
        scratch_shapes=[pltpu.VMEM((NSLOT, per_blk, 2 * SLAB, LANES), BF16),
                        pltpu.SemaphoreType.DMA((NSLOT,))],
        compiler_params=_params(("arbitrary",)),
        name="peer_experts",
    )(idx, idx, h_slab, gate, y_slab, diag, seg, segt, uv)


def _channel_mixer(y, norm_g, w_q, keys, uv):
    n, d = y.shape
    h, idx_t, gate_t = _peer_route(y, norm_g, w_q, keys)
    out = _peer_experts(idx_t.T, gate_t.T, h.reshape(n, SLAB, LANES), y.reshape(n, SLAB, LANES), uv)
    return out.reshape(n, d)


def kernel(x_prompt, x_sample, cache_sb_k, cache_sb_v, cache_df_k, cache_df_v, page_table, norm_mix_g, w_in, df_q_norm_g, df_k_norm_g, lambda_q1, lambda_k1, lambda_q2, lambda_k2, df_subln_g, w_branch_sb, w_branch_df, w_out, norm_ffn_g, peer_w_q, peer_sub_keys, peer_u, peer_v):
    batch, seq, d_model = x_prompt.shape
    dec_batch, dec_seq, _ = x_sample.shape
    depth = w_in.shape[0]
    h_sb = cache_sb_k.shape[3]
    h_df = cache_df_k.shape[3]
    d_sb = h_sb * DH
    d_df = h_df * 2 * DH
    n_pool = cache_sb_k.shape[1]
    heads = peer_sub_keys.shape[1]
    assert d_model == SLAB * LANES and cache_sb_k.shape[2] == PAGE and peer_sub_keys.shape[3] == N_KEYS
    assert h_sb == 8 and 2 * h_df == 8 and dec_seq <= QROWS
    slopes = jnp.asarray(2.0 ** (-8.0 * np.arange(1, h_df + 1) / h_df), F32)
    plain_rows = tuple(range(8))
    half_rows = tuple((c % 2) * h_df + c // 2 for c in range(8))
    pools = (cache_sb_k.reshape(-1, DH), cache_sb_v.reshape(-1, DH), cache_df_k.reshape(-1, DH),
             _half_major(cache_df_v.reshape(-1, d_df), h_df))

    yp = x_prompt.reshape(batch * seq, d_model)
    ys = x_sample.reshape(dec_batch * dec_seq, d_model)
    outs = [[] for _ in range(8)]
    for l in range(depth):
        lambda_init = 0.8 - 0.6 * math.exp(-0.3 * l)
        out_scale = 1.0 - lambda_init
        lam = _lambda(lambda_q1[l], lambda_k1[l], lambda_q2[l], lambda_k2[l], lambda_init)
        wl = w_in[l].astype(BF16)
        cuts = np.cumsum([0, d_sb, d_sb, d_sb, d_df, d_df, d_df, 2 * d_model])
        w_sq, w_sk, w_sv, w_dq, w_dk, w_dv, w_g = [wl[:, a:b] for a, b in zip(cuts[:-1], cuts[1:])]
        w_sb = w_branch_sb[l].astype(BF16)
        w_df = w_branch_df[l].astype(BF16)
        w_o = w_out[l].astype(BF16)
        w_q = peer_w_q[l].astype(BF16)
        keys = peer_sub_keys[l].reshape(2 * heads, N_KEYS, DH).astype(BF16)
        n_exp = peer_u.shape[1]
        uv = jnp.concatenate([peer_u[l].astype(BF16).reshape(n_exp, SLAB, LANES),
                              peer_v[l].astype(BF16).reshape(n_exp, SLAB, LANES)], axis=1)

        def mixer_inputs(x, tag):
            h = _rmsnorm(x, norm_mix_g[l])
            (sq,) = _proj(h, w_sq, dtypes=(BF16,), name=f"proj_sq_{tag}")
            sk32, sk = _proj(h, w_sk, dtypes=(F32, BF16), cache_rows=plain_rows, name=f"proj_sk_{tag}")
            sv32, sv = _proj(h, w_sv, dtypes=(F32, BF16), cache_rows=plain_rows, name=f"proj_sv_{tag}")
            (dq,) = _proj(h, w_dq, dtypes=(BF16,), gain=df_q_norm_g[l], name=f"proj_dq_{tag}")
            dk32, dk = _proj(h, w_dk, dtypes=(F32, BF16), gain=df_k_norm_g[l], cache_rows=plain_rows,
                             name=f"proj_dk_{tag}")
            dv32, dv = _proj(h, w_dv, dtypes=(F32, BF16), cache_rows=half_rows, name=f"proj_dv_{tag}")
            (gates,) = _proj(h, w_g, dtypes=(BF16,), sigmoid=True, name=f"proj_gates_{tag}")
            return (sq, sk, sv, dq, dk, dv, gates), (sk32, sv32, dk32, dv32)

        def mixer_out(x, osb, odf, gates):
            merged = _merge(osb, odf, w_sb, w_df, gates)
            return _resid_matmul(merged, w_o, x)

        (sq, sk, sv, dq, dk, dv, gates), kv32 = mixer_inputs(yp, "p")
        osb = _sb_prompt(sq, sk, sv, batch, seq, h_sb)
        odf = _df_prompt(lam, slopes, dq, dk, dv, df_subln_g[l], batch, seq, h_df, out_scale)
        yp = mixer_out(yp, osb, odf, gates)
        for o, a in zip(outs[:4], kv32):
            o.append(a)

        hs = _rmsnorm(ys, norm_mix_g[l])
        (sq32,) = _proj(hs, w_sq, dtypes=(F32,), name="proj_sq_s")
        (sk32,) = _proj(hs, w_sk, dtypes=(F32,), cache_rows=plain_rows, name="proj_sk_s")
        (sv32,) = _proj(hs, w_sv, dtypes=(F32,), cache_rows=plain_rows, name="proj_sv_s")
        (dq32,) = _proj(hs, w_dq, dtypes=(F32,), gain=df_q_norm_g[l], name="proj_dq_s")
        (dk32,) = _proj(hs, w_dk, dtypes=(F32,), gain=df_k_norm_g[l], cache_rows=plain_rows,
                        name="proj_dk_s")
        (dv32,) = _proj(hs, w_dv, dtypes=(F32,), cache_rows=half_rows, name="proj_dv_s")
        (gates_s,) = _proj(hs, w_g, dtypes=(BF16,), sigmoid=True, name="proj_gates_s")

        def query_rows(a):
            a = a.reshape(dec_batch, dec_seq, a.shape[-1])
            return jnp.pad(a, ((0, 0), (0, QROWS - dec_seq), (0, 0)))

        def key_block(a):
            a = a.reshape(dec_batch, dec_seq * 8, DH)
            return jnp.pad(a, ((0, 0), (0, (PAGE - dec_seq) * 8), (0, 0))).reshape(dec_batch * PAGE * 8, DH)

        def token_rows(a):
            return a[:, :dec_seq].reshape(dec_batch * dec_seq, a.shape[-1]).astype(BF16)

        new_kv = tuple(key_block(a) for a in (sk32, sv32, dk32, dv32))
        osb_s, odf_s = _sample_attention(
            page_table, lam, slopes, query_rows(sq32), query_rows(dq32), df_subln_g[l], new_kv, pools,
            l * n_pool, out_scale=out_scale)
        ys = mixer_out(ys, token_rows(osb_s), token_rows(odf_s), gates_s)
        for o, a in zip(outs[4:], (sk32, sv32, dk32, dv32)):
            o.append(a)

        yp = _channel_mixer(yp, norm_ffn_g[l], w_q, keys, uv)
        ys = _channel_mixer(ys, norm_ffn_g[l], w_q, keys, uv)

    def stack(items, lead, tail):
        a = items[0][None] if depth == 1 else jnp.stack(items)
        if tail[-1] != DH:
            a = a.reshape((depth,) + lead + (2, h_df, DH))
            a = jnp.swapaxes(a, -3, -2)
        return a.reshape((depth,) + lead + tail)

    pl_, sl_ = (batch, seq), (dec_batch, dec_seq)
    return (yp.reshape(batch, seq, d_model), ys.reshape(dec_batch, dec_seq, d_model),
            stack(outs[0], pl_, (h_sb, DH)), stack(outs[1], pl_, (h_sb, DH)),
            stack(outs[2], pl_, (h_df, 2, DH)), stack(outs[3], pl_, (h_df, 2 * DH)),
            stack(outs[4], sl_, (h_sb, DH)), stack(outs[5], sl_, (h_sb, DH)),
            stack(outs[6], sl_, (h_df, 2, DH)), stack(outs[7], sl_, (h_df, 2 * DH)))
```
